```python
import jax, jax.numpy as jnp
from jax import lax
import numpy as np

D_MODEL = 1024
BATCH = 8
SEQ = 4096
DEPTH = 4

N_MIXERS = 3
CONV_WIDTH = 31
NSA_HEADS = 16
NSA_KV_GROUPS = 4
NSA_HEADS_PER_GROUP = NSA_HEADS // NSA_KV_GROUPS
NSA_HEAD_DIM = D_MODEL // NSA_HEADS
CMP_BLOCK = 32
CMP_STRIDE = 16
CMP_HIDDEN = 2 * NSA_HEAD_DIM
SLC_BLOCK = 64
SLC_TOP_N = 16
WINDOW = 512
NSA_QBLOCK = 64
ROPE_THETA = 500000.0
ROT_DIM = NSA_HEAD_DIM // 4
NSA_Q_WIDTH = NSA_HEADS * NSA_HEAD_DIM
NSA_KV_WIDTH = NSA_KV_GROUPS * NSA_HEAD_DIM
NSA_IN_WIDTH = NSA_Q_WIDTH + 6 * NSA_KV_WIDTH + 3 * NSA_HEADS
POOL_WINDOWS = (2, 4, 8, 16)
POOL_GROUP = D_MODEL // len(POOL_WINDOWS)
D_FF = 7 * D_MODEL // 2
N_EXPERTS = 8
TOP_K = 2
MOE_BLOCK = 128
DEEPNORM_ALPHA = (2 * DEPTH) ** 0.25
DEEPNORM_BETA = (8 * DEPTH) ** -0.25
LN_EPS = 1e-5
NEG_INF = -1e30
SEL_BIG = 1e9
N_CONV_LAYERS = (DEPTH + 2) // 3
N_NSA_LAYERS = (DEPTH + 1) // 3
N_POOL_LAYERS = DEPTH // 3
N_DENSE_LAYERS = (DEPTH + 1) // 2
N_MOE_LAYERS = DEPTH // 2

kernel_name = "hybrid_conv_nsa_pool_moe_deepnorm_adaln"


def layer_norm(x, g, b):
    xf = x.astype(jnp.float32)
    mu = xf.mean(-1, keepdims=True)
    var = jnp.square(xf - mu).mean(-1, keepdims=True)
    return ((xf - mu) * lax.rsqrt(var + LN_EPS) * g + b).astype(x.dtype)


def rope_partial(x, pos):
    half = ROT_DIM // 2
    inv = ROPE_THETA ** (-jnp.arange(half, dtype=jnp.float32) * 2.0 / ROT_DIM)
    ang = pos.astype(jnp.float32)[..., None] * inv
    cos, sin = jnp.cos(ang)[:, :, None, :], jnp.sin(ang)[:, :, None, :]
    xr = x[..., :ROT_DIM].astype(jnp.float32)
    x1, x2 = xr[..., :half], xr[..., half:]
    rot = jnp.concatenate([x1 * cos - x2 * sin, x2 * cos + x1 * sin], axis=-1)
    return jnp.concatenate([rot.astype(x.dtype), x[..., ROT_DIM:]], axis=-1)


def masked_softmax(s, mask):
    s = jnp.where(mask, s.astype(jnp.float32), NEG_INF)
    return jnp.where(mask, jax.nn.softmax(s, axis=-1), 0.0)


def conv_module(u, w_in, b_in, dw, dw_b, ln_g, ln_b, w_out, b_out):
    a, g = jnp.split(u @ w_in + b_in, 2, axis=-1)
    h = a * jax.nn.sigmoid(g)
    h = lax.conv_general_dilated(h, dw[:, None, :], window_strides=(1,),
                                 padding=[(CONV_WIDTH - 1, 0)],
                                 dimension_numbers=('NWC', 'WIO', 'NWC'),
                                 feature_group_count=h.shape[-1]) + dw_b
    h = jax.nn.silu(layer_norm(h, ln_g, ln_b))
    return h @ w_out + b_out


def nsa_mixer(u, positions, w_in, cmp_pos, cmp_w1, cmp_w2, w_out):
    B, T, _ = u.shape
    H, G, R, Dh, QB = NSA_HEADS, NSA_KV_GROUPS, NSA_HEADS_PER_GROUP, NSA_HEAD_DIM, NSA_QBLOCK
    splits = [NSA_Q_WIDTH + k * NSA_KV_WIDTH for k in range(7)]
    q, kc, vc, ks, vs, kw, vw, gl = jnp.split(u @ w_in, splits, axis=-1)
    q = rope_partial(q.reshape(B, T, H, Dh), positions)
    kvshape = lambda a: a.reshape(B, T, G, Dh)
    kc, vc, vs, vw = kvshape(kc), kvshape(vc), kvshape(vs), kvshape(vw)
    ks = rope_partial(kvshape(ks), positions)
    kw = rope_partial(kvshape(kw), positions)
    gates = jax.nn.sigmoid(gl.astype(jnp.float32)).reshape(B, T, G, R, 3)

    n_cmp = (T - CMP_BLOCK) // CMP_STRIDE + 1
    cmp_idx = jnp.arange(n_cmp)[:, None] * CMP_STRIDE + jnp.arange(CMP_BLOCK)[None, :]
    cmp_end = cmp_idx[:, -1]

    def compress(a, pos_emb, w1, w2):
        blk = a[:, cmp_idx] + pos_emb[:, None, :]
        blk = blk.transpose(0, 1, 3, 2, 4).reshape(B, n_cmp, G, CMP_BLOCK * Dh)
        return jax.nn.gelu(blk @ w1) @ w2

    k_cmp = rope_partial(compress(kc, cmp_pos[0], cmp_w1[0], cmp_w2[0]), positions[:, cmp_end])
    v_cmp = compress(vc, cmp_pos[1], cmp_w1[1], cmp_w2[1])

    n_slc = T // SLC_BLOCK
    top_n = min(SLC_TOP_N, n_slc)
    sub_per_slc = SLC_BLOCK // CMP_STRIDE
    seg = jnp.arange(n_cmp)
    cmp_to_slc = sum(jax.nn.one_hot((seg + o) // sub_per_slc, n_slc, dtype=jnp.float32)
                     for o in range(CMP_BLOCK // CMP_STRIDE))
    to_blocks = lambda a: a.reshape(B, n_slc, SLC_BLOCK, G, Dh).transpose(0, 3, 1, 2, 4)
    ks_blk, vs_blk = to_blocks(ks), to_blocks(vs)

    pad = ((0, 0), (WINDOW, 0), (0, 0), (0, 0))
    kw_pad, vw_pad = jnp.pad(kw, pad), jnp.pad(vw, pad)

    scale = Dh ** -0.5
    b_ix = jnp.arange(B)[:, None, None, None]
    g_ix = jnp.arange(G)[None, :, None, None]
    blk_ids = jnp.arange(n_slc)

    def query_block(i):
        q0 = i * QB
        t = q0 + jnp.arange(QB)
        qb = lax.dynamic_slice_in_dim(q, q0, QB, axis=1).reshape(B, QB, G, R, Dh)
        gb = lax.dynamic_slice_in_dim(gates, q0, QB, axis=1)
        s = jnp.einsum('bqgrd,bngd->bgrqn', qb, k_cmp) * scale
        p_c = masked_softmax(s, cmp_end[None, :] <= t[:, None])
        o_c = jnp.einsum('bgrqn,bngd->bqgrd', p_c, v_cmp)
        imp = jnp.einsum('bgrqn,nj->bgqj', p_c, cmp_to_slc)
        cur = t // SLC_BLOCK
        forced = (blk_ids == 0) | (blk_ids == cur[:, None]) | (blk_ids == cur[:, None] - 1)
        score = jnp.where(forced, SEL_BIG, jnp.where(blk_ids <= cur[:, None], imp, -SEL_BIG))
        _, sel = lax.top_k(score, top_n)
        k_sel = ks_blk[b_ix, g_ix, sel]
        v_sel = vs_blk[b_ix, g_ix, sel]
        tok = sel[..., None] * SLC_BLOCK + jnp.arange(SLC_BLOCK)
        ok = (sel[..., None] <= cur[:, None, None]) & (tok <= t[:, None, None])
        s = jnp.einsum('bqgrd,bgqnld->bgrqnl', qb, k_sel) * scale
        s = s.reshape(B, G, R, QB, top_n * SLC_BLOCK)
        p_s = masked_softmax(s, ok.reshape(B, G, 1, QB, top_n * SLC_BLOCK))
        p_s = p_s.reshape(B, G, R, QB, top_n, SLC_BLOCK)
        o_s = jnp.einsum('bgrqnl,bgqnld->bqgrd', p_s, v_sel)
        kwb = lax.dynamic_slice_in_dim(kw_pad, q0, QB + WINDOW, axis=1)
        vwb = lax.dynamic_slice_in_dim(vw_pad, q0, QB + WINDOW, axis=1)
        spos = q0 - WINDOW + jnp.arange(QB + WINDOW)
        dlt = t[:, None] - spos[None, :]
        ok_w = (dlt >= 0) & (dlt < WINDOW) & (spos[None, :] >= 0)
        s = jnp.einsum('bqgrd,bkgd->bgrqk', qb, kwb) * scale
        p_w = masked_softmax(s, ok_w)
        o_w = jnp.einsum('bgrqk,bkgd->bqgrd', p_w, vwb)
        o = gb[..., 0:1] * o_c + gb[..., 1:2] * o_s + gb[..., 2:3] * o_w
        return o.reshape(B, QB, H * Dh).astype(u.dtype)

    o = lax.map(query_block, jnp.arange(T // QB))
    o = o.transpose(1, 0, 2, 3).reshape(B, T, H * Dh)
    return o @ w_out


def pool_mixer(u, w_grp, scale):
    B, T, D = u.shape
    uf = u.astype(jnp.float32)
    csum = jnp.pad(jnp.cumsum(uf, axis=1), ((0, 0), (1, 0), (0, 0)))
    t = jnp.arange(T)
    groups = []
    for gi, w in enumerate(POOL_WINDOWS):
        lo = jnp.maximum(t + 1 - w, 0)
        sl = slice(gi * POOL_GROUP, (gi + 1) * POOL_GROUP)
        cnt = (t + 1 - lo).astype(jnp.float32)[None, :, None]
        mean = (csum[:, 1:, sl] - csum[:, lo, sl]) / cnt
        groups.append(mean - uf[:, :, sl])
    p = jnp.stack(groups, axis=2).astype(u.dtype)
    y = jnp.einsum('btgc,gcd->btgd', p, w_grp).reshape(B, T, D)
    return y * scale


def swiglu(u, w1, w3, w2):
    return (jax.nn.silu(u @ w1) * (u @ w3)) @ w2


def moe_ffn(u, router, w1, w3, w2):
    B, T, D = u.shape
    xf = u.reshape(-1, D)
    N = xf.shape[0]
    logits = (xf @ router).astype(jnp.float32)
    top_v, top_e = lax.top_k(logits, TOP_K)
    gate = jax.nn.softmax(top_v, axis=-1)
    e_flat = top_e.reshape(-1)
    tok_flat = jnp.repeat(jnp.arange(N, dtype=jnp.int32), TOP_K)
    g_flat = gate.reshape(-1)
    order = jnp.argsort(e_flat)
    e_s, tok_s, g_s = e_flat[order], tok_flat[order], g_flat[order]
    sizes = jnp.bincount(e_flat, length=N_EXPERTS)
    start = jnp.cumsum(sizes) - sizes
    padded = (sizes + MOE_BLOCK - 1) // MOE_BLOCK * MOE_BLOCK
    pend = jnp.cumsum(padded)
    pstart = pend - padded
    dest = pstart[e_s] + (jnp.arange(N * TOP_K) - start[e_s])
    cap = N * TOP_K + N_EXPERTS * MOE_BLOCK
    n_blk = cap // MOE_BLOCK
    slot_tok = jnp.zeros((cap,), jnp.int32).at[dest].set(tok_s)
    slot_gate = jnp.zeros((cap,), jnp.float32).at[dest].set(g_s)
    blk_exp = jnp.minimum(jnp.searchsorted(pend, jnp.arange(n_blk) * MOE_BLOCK, side='right'),
                          N_EXPERTS - 1)

    def run_block(args):
        tok, e = args
        h = xf[tok]
        return (jax.nn.silu(h @ w1[e]) * (h @ w3[e])) @ w2[e]

    y = lax.map(run_block, (slot_tok.reshape(n_blk, MOE_BLOCK), blk_exp))
    y = y.reshape(cap, D) * slot_gate[:, None].astype(y.dtype)
    return jnp.zeros_like(xf).at[slot_tok].add(y).reshape(B, T, D)


def setup_inputs(seed: int = 0) -> dict:
    key = jax.random.key(seed)
    keys = iter(jax.random.split(key, 48))

    def nrm(shape, scale):
        return scale * jax.random.normal(next(keys), shape, jnp.float32)

    D, F, E, Dh = D_MODEL, D_FF, N_EXPERTS, NSA_HEAD_DIM
    nA, nB, nC, nD, nM = N_CONV_LAYERS, N_NSA_LAYERS, N_POOL_LAYERS, N_DENSE_LAYERS, N_MOE_LAYERS
    beta = DEEPNORM_BETA
    x = nrm((BATCH, SEQ, D), 1.0)
    c = nrm((BATCH, D), 1.0)
    positions = (jnp.arange(SEQ, dtype=jnp.int32)[None, :]
                 + jax.random.randint(next(keys), (BATCH, 1), 0, 1024, dtype=jnp.int32))
    return {
        "x": x,
        "c": c,
        "positions": positions,
        "ada_w": nrm((DEPTH, D, 6 * D), 0.1 * D ** -0.5),
        "ada_b": nrm((DEPTH, 6 * D), 0.02),
        "ln_g": 1.0 + nrm((DEPTH, 2, D), 0.02),
        "ln_b": nrm((DEPTH, 2, D), 0.02),
        "conv_w_in": nrm((nA, D, 2 * D), D ** -0.5),
        "conv_b_in": nrm((nA, 2 * D), 0.02),
        "conv_dw": nrm((nA, CONV_WIDTH, D), CONV_WIDTH ** -0.5),
        "conv_dw_b": nrm((nA, D), 0.02),
        "conv_ln_g": 1.0 + nrm((nA, D), 0.02),
        "conv_ln_b": nrm((nA, D), 0.02),
        "conv_w_out": nrm((nA, D, D), beta * D ** -0.5),
        "conv_b_out": nrm((nA, D), 0.02),
        "nsa_w_in": nrm((nB, D, NSA_IN_WIDTH), D ** -0.5),
        "nsa_cmp_pos": nrm((nB, 2, CMP_BLOCK, Dh), 0.1),
        "nsa_cmp_w1": nrm((nB, 2, CMP_BLOCK * Dh, CMP_HIDDEN), (CMP_BLOCK * Dh) ** -0.5),
        "nsa_cmp_w2": nrm((nB, 2, CMP_HIDDEN, Dh), CMP_HIDDEN ** -0.5),
        "nsa_w_out": nrm((nB, NSA_Q_WIDTH, D), beta * NSA_Q_WIDTH ** -0.5),
        "pool_w": nrm((nC, len(POOL_WINDOWS), POOL_GROUP, POOL_GROUP), beta * POOL_GROUP ** -0.5),
        "pool_scale": 1.0 + nrm((nC, D), 0.1),
        "ffn_w1": nrm((nD, D, F), D ** -0.5),
        "ffn_w3": nrm((nD, D, F), D ** -0.5),
        "ffn_w2": nrm((nD, F, D), beta * F ** -0.5),
        "moe_router": nrm((nM, D, E), D ** -0.5),
        "moe_w1": nrm((nM, E, D, F), D ** -0.5),
        "moe_w3": nrm((nM, E, D, F), D ** -0.5),
        "moe_w2": nrm((nM, E, F, D), beta * F ** -0.5),
    }


def reference(x, c, positions, ada_w, ada_b, ln_g, ln_b,
              conv_w_in, conv_b_in, conv_dw, conv_dw_b, conv_ln_g, conv_ln_b, conv_w_out, conv_b_out,
              nsa_w_in, nsa_cmp_pos, nsa_cmp_w1, nsa_cmp_w2, nsa_w_out,
              pool_w, pool_scale,
              ffn_w1, ffn_w3, ffn_w2,
              moe_router, moe_w1, moe_w3, moe_w2):
    cond = jax.nn.silu(c)
    for i in range(DEPTH):
        mod = (cond @ ada_w[i] + ada_b[i])[:, None, :]
        sh_m, sc_m, g_m, sh_f, sc_f, g_f = jnp.split(mod, 6, axis=-1)
        u = x * (1 + sc_m) + sh_m
        j = i // N_MIXERS
        if i % N_MIXERS == 0:
            y = conv_module(u, conv_w_in[j], conv_b_in[j], conv_dw[j], conv_dw_b[j],
                            conv_ln_g[j], conv_ln_b[j], conv_w_out[j], conv_b_out[j])
        elif i % N_MIXERS == 1:
            y = nsa_mixer(u, positions, nsa_w_in[j], nsa_cmp_pos[j], nsa_cmp_w1[j],
                          nsa_cmp_w2[j], nsa_w_out[j])
        else:
            y = pool_mixer(u, pool_w[j], pool_scale[j])
        x = layer_norm(DEEPNORM_ALPHA * x + (1 + g_m) * y, ln_g[i, 0], ln_b[i, 0])
        u = x * (1 + sc_f) + sh_f
        j = i // 2
        if i % 2 == 0:
            y = swiglu(u, ffn_w1[j], ffn_w3[j], ffn_w2[j])
        else:
            y = moe_ffn(u, moe_router[j], moe_w1[j], moe_w3[j], moe_w2[j])
        x = layer_norm(DEEPNORM_ALPHA * x + (1 + g_f) * y, ln_g[i, 1], ln_b[i, 1])
    return x
```

```python
import functools

import jax
import jax.numpy as jnp
from jax import lax
from jax.experimental import pallas as pl
from jax.experimental.pallas import tpu as pltpu

F32 = jnp.float32
BF16 = jnp.bfloat16
HIGHEST = lax.Precision.HIGHEST

N_MIXERS = 3
CONV_WIDTH = 31
NSA_HEADS = 16
NSA_KV_GROUPS = 4
NSA_HEADS_PER_GROUP = NSA_HEADS // NSA_KV_GROUPS
NSA_HEAD_DIM = 64
CMP_BLOCK = 32
CMP_STRIDE = 16
SLC_BLOCK = 64
SLC_TOP_N = 16
WINDOW = 512
ROPE_THETA = 500000.0
ROT_DIM = NSA_HEAD_DIM // 4
POOL_WINDOWS = (2, 4, 8, 16)
N_EXPERTS = 8
TOP_K = 2
LN_EPS = 1e-5
NEG_INF = -1e30
SEL_BIG = 1e9

LANES = 128
VMEM_LIMIT = 56 * 1024 * 1024


def _cparams(*sem):
    return pltpu.CompilerParams(dimension_semantics=sem, vmem_limit_bytes=VMEM_LIMIT)


def _layer_norm(v, g, b):
    mu = jnp.mean(v, axis=-1, keepdims=True)
    d = v - mu
    var = jnp.mean(d * d, axis=-1, keepdims=True)
    return d * lax.rsqrt(var + LN_EPS) * g + b


def _silu(v):
    return v * jax.nn.sigmoid(v)


def _row(v):
    return v.reshape(1, -1)


def _ada_kernel(c_ref, w_ref, b_ref, o_ref):
    cond = _silu(c_ref[...])
    o_ref[0] = jnp.dot(cond, w_ref[0], precision=HIGHEST, preferred_element_type=F32) + b_ref[0]


def ada_mod(c, ada_w, ada_b):
    depth, d, d6 = ada_w.shape
    b = c.shape[0]
    tn = d6 // 4
    return pl.pallas_call(
        _ada_kernel,
        grid=(depth, d6 // tn),
        in_specs=[
            pl.BlockSpec((b, d), lambda i, j: (0, 0)),
            pl.BlockSpec((1, d, tn), lambda i, j: (i, 0, j)),
            pl.BlockSpec((1, 1, tn), lambda i, j: (i, 0, j)),
        ],
        out_specs=pl.BlockSpec((1, b, tn), lambda i, j: (i, 0, j)),
        out_shape=jax.ShapeDtypeStruct((depth, b, d6), F32),
        compiler_params=_cparams("parallel", "parallel"),
        name="ada_mod",
    )(c, ada_w, ada_b.reshape(depth, 1, d6))


def _mod_spec(d, tiles_per_batch, chunk, ngrid):
    if ngrid == 1:
        return pl.BlockSpec((1, 1, d), lambda i: (i // tiles_per_batch, 0, chunk))
    return pl.BlockSpec((1, 1, d), lambda i, j: (i // tiles_per_batch, 0, chunk))


def _conv_in_kernel(x_ref, sh_ref, sc_ref, wa_ref, wg_ref, ba_ref, bg_ref, h_ref, u_ref):
    @pl.when(pl.program_id(1) == 0)
    def _():
        u_ref[...] = (x_ref[...] * (1.0 + sc_ref[0]) + sh_ref[0]).astype(BF16)

    u = u_ref[...]
    a = jnp.dot(u, wa_ref[...], preferred_element_type=F32) + ba_ref[...]
    g = jnp.dot(u, wg_ref[...], preferred_element_type=F32) + bg_ref[...]
    h_ref[...] = a * jax.nn.sigmoid(g)


def conv_in(x2, mod3, w_in_bf, b_in, seq, tm=512, tn=512):
    n, d = x2.shape
    tpb = seq // tm
    nj = d // tn
    b2 = _row(b_in)
    return pl.pallas_call(
        _conv_in_kernel,
        grid=(n // tm, nj),
        in_specs=[
            pl.BlockSpec((tm, d), lambda i, j: (i, 0)),
            _mod_spec(d, tpb, 0, 2),
            _mod_spec(d, tpb, 1, 2),
            pl.BlockSpec((d, tn), lambda i, j: (0, j)),
            pl.BlockSpec((d, tn), lambda i, j: (0, j + nj)),
            pl.BlockSpec((1, tn), lambda i, j: (0, j)),
            pl.BlockSpec((1, tn), lambda i, j: (0, j + nj)),
        ],
        out_specs=pl.BlockSpec((tm, tn), lambda i, j: (i, j)),
        out_shape=jax.ShapeDtypeStruct((n, d), F32),
        scratch_shapes=[pltpu.VMEM((tm, d), BF16)],
        compiler_params=_cparams("parallel", "arbitrary"),
        name="conv_in",
    )(x2, mod3, mod3, w_in_bf, w_in_bf, b2, b2)


CONV_HALO = 32
CONV_ROWS = 32


def _conv_tail_kernel(tpb, alpha, hc_ref, hp_ref, x_ref, gm_ref, dw_ref, dwb_ref, cg_ref, cb_ref,
                      wo_ref, bo_ref, lg_ref, lb_ref, o_ref, win_ref, cv_ref):
    tm = hc_ref.shape[0]
    first = (pl.program_id(0) % tpb) == 0
    win_ref[0:CONV_HALO, :] = jnp.where(first, 0.0, hp_ref[...])
    win_ref[CONV_HALO:CONV_HALO + tm, :] = hc_ref[...]
    off = CONV_HALO - (CONV_WIDTH - 1)
    for rc in range(tm // CONV_ROWS):
        r0 = rc * CONV_ROWS
        acc = win_ref[r0 + off:r0 + off + CONV_ROWS, :] * dw_ref[0:1, :] + dwb_ref[...]
        for k in range(1, CONV_WIDTH):
            acc = acc + win_ref[r0 + off + k:r0 + off + k + CONV_ROWS, :] * dw_ref[k:k + 1, :]
        cv_ref[r0:r0 + CONV_ROWS, :] = acc
    hn = _silu(_layer_norm(cv_ref[...], cg_ref[...], cb_ref[...]))
    y = jnp.dot(hn.astype(BF16), wo_ref[...], preferred_element_type=F32) + bo_ref[...]
    o_ref[...] = _layer_norm(alpha * x_ref[...] + (1.0 + gm_ref[0]) * y, lg_ref[...], lb_ref[...])


def conv_tail(h, x2, mod3, dw, dw_b, cln_g, cln_b, w_out_bf, b_out, ln_g, ln_b, seq, alpha, tm=256):
    n, d = x2.shape
    tpb = seq // tm
    hb = tm // CONV_HALO
    full = lambda shape: pl.BlockSpec(shape, lambda i: (0, 0))
    return pl.pallas_call(
        functools.partial(_conv_tail_kernel, tpb, alpha),
        grid=(n // tm,),
        in_specs=[
            pl.BlockSpec((tm, d), lambda i: (i, 0)),
            pl.BlockSpec((CONV_HALO, d), lambda i: (jnp.maximum(i * hb - 1, 0), 0)),
            pl.BlockSpec((tm, d), lambda i: (i, 0)),
            _mod_spec(d, tpb, 2, 1),
            full((CONV_WIDTH, d)), full((1, d)), full((1, d)), full((1, d)),
            full((d, d)), full((1, d)), full((1, d)), full((1, d)),
        ],
        out_specs=pl.BlockSpec((tm, d), lambda i: (i, 0)),
        out_shape=jax.ShapeDtypeStruct((n, d), F32),
        scratch_shapes=[pltpu.VMEM((tm + CONV_HALO, d), F32), pltpu.VMEM((tm, d), F32)],
        compiler_params=_cparams("parallel"),
        name="conv_tail",
    )(h, h, x2, mod3, dw, _row(dw_b), _row(cln_g), _row(cln_b), w_out_bf, _row(b_out), _row(ln_g), _row(ln_b))


def _ffn_kernel(alpha, x_ref, sh_ref, sc_ref, gf_ref, w1_ref, w3_ref, w2_ref, lg_ref, lb_ref,
                o_ref, u_ref, acc_ref):
    f = pl.program_id(1)

    @pl.when(f == 0)
    def _():
        u_ref[...] = (x_ref[...] * (1.0 + sc_ref[0]) + sh_ref[0]).astype(BF16)

    u = u_ref[...]
    h = _silu(jnp.dot(u, w1_ref[...], preferred_element_type=F32)) * jnp.dot(
        u, w3_ref[...], preferred_element_type=F32)
    part = jnp.dot(h.astype(BF16), w2_ref[...], preferred_element_type=F32)

    @pl.when(f == 0)
    def _():
        acc_ref[...] = part

    @pl.when(f > 0)
    def _():
        acc_ref[...] += part

    @pl.when(f == pl.num_programs(1) - 1)
    def _():
        o_ref[...] = _layer_norm(alpha * x_ref[...] + (1.0 + gf_ref[0]) * acc_ref[...],
                                 lg_ref[...], lb_ref[...])


def ffn_dense(x2, mod3, w1_bf, w3_bf, w2_bf, ln_g, ln_b, seq, alpha, tm=1024, tf=512):
    n, d = x2.shape
    ff = w1_bf.shape[1]
    tpb = seq // tm
    return pl.pallas_call(
        functools.partial(_ffn_kernel, alpha),
        grid=(n // tm, ff // tf),
        in_specs=[
            pl.BlockSpec((tm, d), lambda i, f: (i, 0)),
            _mod_spec(d, tpb, 3, 2), _mod_spec(d, tpb, 4, 2), _mod_spec(d, tpb, 5, 2),
            pl.BlockSpec((d, tf), lambda i, f: (0, f)),
            pl.BlockSpec((d, tf), lambda i, f: (0, f)),
            pl.BlockSpec((tf, d), lambda i, f: (f, 0)),
            pl.BlockSpec((1, d), lambda i, f: (0, 0)),
            pl.BlockSpec((1, d), lambda i, f: (0, 0)),
        ],
        out_specs=pl.BlockSpec((tm, d), lambda i, f: (i, 0)),
        out_shape=jax.ShapeDtypeStruct((n, d), F32),
        scratch_shapes=[pltpu.VMEM((tm, d), BF16), pltpu.VMEM((tm, d), F32)],
        compiler_params=_cparams("parallel", "arbitrary"),
        name="ffn_dense",
    )(x2, mod3, mod3, mod3, w1_bf, w3_bf, w2_bf, _row(ln_g), _row(ln_b))


POOL_HALO = 16


def _pool_kernel(tpb, alpha, xc_ref, xp_ref, sh_ref, sc_ref, gm_ref, w_ref, ps_ref, lg_ref, lb_ref,
                 o_ref, win_ref):
    tm, d = xc_ref.shape
    gw = d // len(POOL_WINDOWS)
    tile = pl.program_id(0) % tpb
    first = tile == 0
    scale = 1.0 + sc_ref[0]
    up = xp_ref[...] * scale + sh_ref[0]
    uc = xc_ref[...] * scale + sh_ref[0]
    win_ref[0:POOL_HALO, :] = jnp.where(first, 0.0, up)
    win_ref[POOL_HALO:POOL_HALO + tm, :] = uc
    t = tile * tm + lax.broadcasted_iota(jnp.int32, (tm, 1), 0)
    ys = []
    for gi, w in enumerate(POOL_WINDOWS):
        c0 = gi * gw
        s = uc[:, c0:c0 + gw]
        for k in range(1, w):
            s = s + win_ref[POOL_HALO - k:POOL_HALO - k + tm, c0:c0 + gw]
        cnt = jnp.minimum(t + 1, w).astype(F32)
        p = s / cnt - uc[:, c0:c0 + gw]
        ys.append(jnp.dot(p.astype(BF16), w_ref[gi], preferred_element_type=F32))
    y = jnp.concatenate(ys, axis=-1) * ps_ref[...]
    o_ref[...] = _layer_norm(alpha * xc_ref[...] + (1.0 + gm_ref[0]) * y, lg_ref[...], lb_ref[...])


def pool_layer(x2, mod3, w_grp_bf, pool_scale, ln_g, ln_b, seq, alpha, tm=512):
    n, d = x2.shape
    tpb = seq // tm
    hb = tm // POOL_HALO
    ng, gw, _ = w_grp_bf.shape
    return pl.pallas_call(
        functools.partial(_pool_kernel, tpb, alpha),
        grid=(n // tm,),
        in_specs=[
            pl.BlockSpec((tm, d), lambda i: (i, 0)),
            pl.BlockSpec((POOL_HALO, d), lambda i: (jnp.maximum(i * hb - 1, 0), 0)),
            _mod_spec(d, tpb, 0, 1), _mod_spec(d, tpb, 1, 1), _mod_spec(d, tpb, 2, 1),
            pl.BlockSpec((ng, gw, gw), lambda i: (0, 0, 0)),
            pl.BlockSpec((1, d), lambda i: (0, 0)),
            pl.BlockSpec((1, d), lambda i: (0, 0)),
            pl.BlockSpec((1, d), lambda i: (0, 0)),
        ],
        out_specs=pl.BlockSpec((tm, d), lambda i: (i, 0)),
        out_shape=jax.ShapeDtypeStruct((n, d), F32),
        scratch_shapes=[pltpu.VMEM((tm + POOL_HALO, d), F32)],
        compiler_params=_cparams("parallel"),
        name="pool_layer",
    )(x2, x2, mod3, mod3, mod3, w_grp_bf, _row(pool_scale), _row(ln_g), _row(ln_b))


GW = NSA_HEADS_PER_GROUP * NSA_HEAD_DIM
PLAIN, ROPE, SIGMOID = 0, 1, 2


def _rope_tables(positions):
    half = ROT_DIM // 2
    inv = ROPE_THETA ** (-jnp.arange(half, dtype=F32) * 2.0 / ROT_DIM)
    ang = positions.astype(F32)[..., None] * inv
    cos, sin = jnp.cos(ang), jnp.sin(ang)
    l64 = jnp.arange(LANES) % NSA_HEAD_DIM
    idx = l64 % half
    lo, hi = l64 < half, (l64 >= half) & (l64 < ROT_DIM)
    ct = jnp.where(l64 < ROT_DIM, cos[..., idx], 1.0)
    sp = jnp.where(hi, sin[..., idx], 0.0)
    sm = jnp.where(lo, -sin[..., idx], 0.0)
    flat = lambda a: a.reshape(-1, LANES)
    return flat(ct), flat(sp), flat(sm)


def _rope(v, ct, sp, sm):
    half = ROT_DIM // 2
    outs = []
    for c0 in range(0, v.shape[1], LANES):
        vc = v[:, c0:c0 + LANES]
        outs.append(vc * ct + pltpu.roll(vc, half, 1) * sp + pltpu.roll(vc, LANES - half, 1) * sm)
    return outs[0] if len(outs) == 1 else jnp.concatenate(outs, axis=1)


def _proj_kernel(kinds, x_ref, sh_ref, sc_ref, w_ref, ct_ref, sp_ref, sm_ref, o_ref, u_ref):
    j = pl.program_id(1)

    @pl.when(j == 0)
    def _():
        u_ref[...] = (x_ref[...] * (1.0 + sc_ref[0]) + sh_ref[0]).astype(BF16)

    acc = jnp.dot(u_ref[...], w_ref[...], preferred_element_type=F32)

    def is_kind(kind):
        hit = False
        for jj, kk in enumerate(kinds):
            if kk == kind:
                hit = (j == jj) | hit
        return hit

    if ROPE in kinds:
        @pl.when(is_kind(ROPE))
        def _():
            o_ref[...] = _rope(acc, ct_ref[...], sp_ref[...], sm_ref[...]).astype(o_ref.dtype)
    if SIGMOID in kinds:
        @pl.when(is_kind(SIGMOID))
        def _():
            o_ref[...] = jax.nn.sigmoid(acc).astype(o_ref.dtype)
    if PLAIN in kinds:
        @pl.when(is_kind(PLAIN))
        def _():
            o_ref[...] = acc.astype(o_ref.dtype)


def nsa_proj(x2, mod3, w_bf, tables, kinds, out_dtype, seq, name, tm=512):
    n, d = x2.shape
    tn = GW
    tpb = seq // tm
    assert w_bf.shape[1] == tn * len(kinds)
    tab = pl.BlockSpec((tm, LANES), lambda i, j: (i, 0))
    return pl.pallas_call(
        functools.partial(_proj_kernel, kinds),
        grid=(n // tm, len(kinds)),
        in_specs=[
            pl.BlockSpec((tm, d), lambda i, j: (i, 0)),
            _mod_spec(d, tpb, 0, 2), _mod_spec(d, tpb, 1, 2),
            pl.BlockSpec((d, tn), lambda i, j: (0, j)),
            tab, tab, tab,
        ],
        out_specs=pl.BlockSpec((tm, tn), lambda i, j: (i, j)),
        out_shape=jax.ShapeDtypeStruct((n, w_bf.shape[1]), out_dtype),
        scratch_shapes=[pltpu.VMEM((tm, d), BF16)],
        compiler_params=_cparams("parallel", "arbitrary"),
        name=name,
    )(x2, mod3, mod3, w_bf, *tables)


def _gelu_tanh(v):
    return 0.5 * v * (1.0 + jnp.tanh(0.7978845608028654 * (v + 0.044715 * (v * v * v))))


def _compress_kernel(xl_ref, xh_ref, w1_ref, w2_ref, pa_ref, pb_ref, ct_ref, sp_ref, sm_ref, o_ref):
    nh = o_ref.shape[2]
    a = None
    bm = None
    for p in range(CMP_STRIDE):
        hp = jnp.concatenate([xl_ref[pl.ds(p, nh, stride=CMP_STRIDE), :],
                              xh_ref[pl.ds(p, nh, stride=CMP_STRIDE), :]], axis=1)
        da = jnp.dot((hp + pa_ref[0, p:p + 1, :]).astype(BF16), w1_ref[0, p], preferred_element_type=F32)
        db = jnp.dot((hp + pb_ref[0, p:p + 1, :]).astype(BF16), w1_ref[0, CMP_STRIDE + p],
                     preferred_element_type=F32)
        a = da if a is None else a + da
        bm = db if bm is None else bm + db
    hid = _gelu_tanh(a + pltpu.roll(bm, nh - 1, 0))
    out = jnp.dot(hid.astype(BF16), w2_ref[0], preferred_element_type=F32)

    @pl.when(pl.program_id(1) == 0)
    def _():
        o_ref[0, 0] = _rope(out, ct_ref[0], sp_ref[0], sm_ref[0]).astype(o_ref.dtype)

    @pl.when(pl.program_id(1) != 0)
    def _():
        o_ref[0, 0] = out.astype(o_ref.dtype)


def nsa_compress(aux, w1bd, w2bd, pos_a, pos_b, ctabs, bsz, seq):
    nh = seq // CMP_STRIDE
    hid = w1bd.shape[-1]
    ctab = pl.BlockSpec((1, nh, LANES), lambda b, s: (b, 0, 0))
    return pl.pallas_call(
        _compress_kernel,
        grid=(bsz, 2),
        in_specs=[
            pl.BlockSpec((seq, LANES), lambda b, s: (b, 2 * s)),
            pl.BlockSpec((seq, LANES), lambda b, s: (b, 2 * s + 1)),
            pl.BlockSpec((1, CMP_BLOCK, GW, hid), lambda b, s: (s, 0, 0, 0)),
            pl.BlockSpec((1, hid, GW), lambda b, s: (s, 0, 0)),
            pl.BlockSpec((1, CMP_STRIDE, GW), lambda b, s: (s, 0, 0)),
            pl.BlockSpec((1, CMP_STRIDE, GW), lambda b, s: (s, 0, 0)),
            ctab, ctab, ctab,
        ],
        out_specs=pl.BlockSpec((1, 1, nh, GW), lambda b, s: (s, b, 0, 0)),
        out_shape=jax.ShapeDtypeStruct((2, bsz, nh, GW), BF16),
        compiler_params=_cparams("parallel", "arbitrary"),
        name="nsa_compress",
    )(aux, aux, w1bd, w2bd, pos_a, pos_b, *ctabs)


def _lane_group(width):
    return lax.broadcasted_iota(jnp.int32, (1, width), 1) // NSA_HEAD_DIM


def _stack_heads(q):
    grp = _lane_group(q.shape[1])
    return jnp.concatenate([jnp.where(grp == r, q, jnp.zeros_like(q)) for r in range(NSA_HEADS_PER_GROUP)],
                           axis=0)


def _merge_heads(o, tq):
    grp = _lane_group(o.shape[1])
    out = jnp.where(grp == 0, o[0:tq], 0.0)
    for r in range(1, NSA_HEADS_PER_GROUP):
        out = out + jnp.where(grp == r, o[r * tq:(r + 1) * tq], 0.0)
    return out


def _nt_dot(a, b):
    return lax.dot_general(a, b, (((1,), (1,)), ((), ())), preferred_element_type=F32)


def _tn_dot(a, b, precision=None):
    return lax.dot_general(a, b, (((0,), (0,)), ((), ())), precision=precision,
                           preferred_element_type=F32)


def _cmp_sel_kernel(q_ref, kc_ref, vc_ref, rep_ref, g_ref, gsel_ref, c2s_ref, oc_ref, sel_ref):
    tq = q_ref.shape[0]
    nc = kc_ref.shape[2]
    r4 = NSA_HEADS_PER_GROUP
    q0 = pl.program_id(2) * tq
    scale = NSA_HEAD_DIM ** -0.5
    qm = _stack_heads((q_ref[...].astype(F32) * scale).astype(BF16))
    krep = jnp.dot(kc_ref[0, 0], rep_ref[0], preferred_element_type=F32).astype(BF16)
    vrep = jnp.dot(vc_ref[0, 0], rep_ref[0], preferred_element_type=F32).astype(BF16)

    s = _nt_dot(qm, krep)
    t_rows = q0 + lax.broadcasted_iota(jnp.int32, (r4 * tq, 1), 0) % tq
    cend = lax.broadcasted_iota(jnp.int32, (1, nc), 1) * CMP_STRIDE + (CMP_BLOCK - 1)
    ok = cend <= t_rows
    s = jnp.where(ok, s, NEG_INF)
    e = jnp.exp(s - jnp.max(s, axis=-1, keepdims=True))
    p = jnp.where(ok, e / jnp.sum(e, axis=-1, keepdims=True), 0.0)
    oc = _merge_heads(jnp.dot(p.astype(BF16), vrep, preferred_element_type=F32), tq)
    gate = jnp.dot(g_ref[...], gsel_ref[0, 0], precision=HIGHEST, preferred_element_type=F32)
    oc_ref[...] = gate * oc

    st = _nt_dot(krep, qm)
    t_lanes = q0 + lax.broadcasted_iota(jnp.int32, (1, r4 * tq), 1) % tq
    cend_s = lax.broadcasted_iota(jnp.int32, (nc, 1), 0) * CMP_STRIDE + (CMP_BLOCK - 1)
    okt = cend_s <= t_lanes
    st = jnp.where(okt, st, NEG_INF)
    et = jnp.exp(st - jnp.max(st, axis=0, keepdims=True))
    pt = jnp.where(okt, et / jnp.sum(et, axis=0, keepdims=True), 0.0)
    psum = pt[:, 0:tq]
    for r in range(1, r4):
        psum = psum + pt[:, r * tq:(r + 1) * tq]
    imp = jnp.dot(c2s_ref[...], psum, precision=HIGHEST, preferred_element_type=F32)
    ns = imp.shape[0]
    jblk = lax.broadcasted_iota(jnp.int32, (ns, 1), 0)
    cur = (q0 + lax.broadcasted_iota(jnp.int32, (1, tq), 1)) // SLC_BLOCK
    forced = (jblk == 0) | (jblk == cur) | (jblk == cur - 1)
    valid = jblk <= cur
    score = jnp.where(forced, SEL_BIG, jnp.where(valid, imp, -SEL_BIG))
    cnt = jnp.zeros((ns, tq), F32)
    for i in range(ns):
        si = score[i:i + 1, :]
        ahead = (si > score) | ((si == score) & (jblk > i))
        cnt = cnt + jnp.where(ahead, 1.0, 0.0)
    top_n = min(SLC_TOP_N, ns)
    sel_ref[0, 0] = jnp.where((cnt < top_n) & valid, 1.0, 0.0).astype(sel_ref.dtype)


def nsa_cmp_sel(qkv, kvcmp, aux, repsel, gsel, c2s_t, bsz, seq, tq=256):
    n = qkv.shape[0]
    g = NSA_KV_GROUPS
    nq = seq // tq
    nc = kvcmp.shape[2]
    ns = seq // SLC_BLOCK
    row = lambda b, gi, i: b * nq + i
    return pl.pallas_call(
        _cmp_sel_kernel,
        grid=(bsz, g, nq),
        in_specs=[
            pl.BlockSpec((tq, GW), lambda b, gi, i: (row(b, gi, i), gi)),
            pl.BlockSpec((1, 1, nc, GW), lambda b, gi, i: (0, b, 0, 0)),
            pl.BlockSpec((1, 1, nc, GW), lambda b, gi, i: (1, b, 0, 0)),
            pl.BlockSpec((1, GW, GW), lambda b, gi, i: (gi, 0, 0)),
            pl.BlockSpec((tq, GW), lambda b, gi, i: (row(b, gi, i), 2)),
            pl.BlockSpec((1, 1, GW, GW), lambda b, gi, i: (0, gi, 0, 0)),
            pl.BlockSpec((ns, nc), lambda b, gi, i: (0, 0)),
        ],
        out_specs=[
            pl.BlockSpec((tq, GW), lambda b, gi, i: (row(b, gi, i), gi)),
            pl.BlockSpec((1, 1, ns, tq), lambda b, gi, i: (b, gi, 0, i)),
        ],
        out_shape=[
            jax.ShapeDtypeStruct((n, g * GW), F32),
            jax.ShapeDtypeStruct((bsz, g, ns, seq), BF16),
        ],
        compiler_params=_cparams("parallel", "parallel", "parallel"),
        name="nsa_cmp_sel",
    )(qkv, kvcmp, kvcmp, repsel, aux, gsel, c2s_t)


def _flash_step(qm, k, v, rep, ok4, m, l, acc):
    krep = jnp.dot(k, rep, preferred_element_type=F32).astype(BF16)
    vrep = jnp.dot(v, rep, preferred_element_type=F32).astype(BF16)
    s = jnp.where(ok4, _nt_dot(qm, krep), NEG_INF)
    m_new = jnp.maximum(m, jnp.max(s, axis=-1, keepdims=True))
    a = jnp.exp(m - m_new)
    p = jnp.where(ok4, jnp.exp(s - m_new), 0.0)
    l_new = a * l + jnp.sum(p, axis=-1, keepdims=True)
    acc_new = a * acc + jnp.dot(p.astype(BF16), vrep, preferred_element_type=F32)
    return m_new, l_new, acc_new


def _attn_kernel(tk, q_ref, ks_ref, vs_ref, kw_ref, vw_ref, rep_ref, sel_ref, e_ref, g_ref,
                 gs_ref, gw_ref, oc_ref, o_ref):
    tq = q_ref.shape[0]
    r4 = NSA_HEADS_PER_GROUP
    qi = pl.program_id(2)
    q0 = qi * tq
    scale = NSA_HEAD_DIM ** -0.5
    qm = _stack_heads((q_ref[...].astype(F32) * scale).astype(BF16))
    rep = rep_ref[0]
    t_col = q0 + lax.broadcasted_iota(jnp.int32, (tq, 1), 0)
    lane = lax.broadcasted_iota(jnp.int32, (1, tk), 1)
    tile4 = lambda ok: jnp.concatenate([ok] * r4, axis=0)
    init = (jnp.full((r4 * tq, 1), NEG_INF, F32), jnp.zeros((r4 * tq, 1), F32),
            jnp.zeros((r4 * tq, GW), F32))
    sel_t = sel_ref[0, 0]

    def sel_body(kt, carry):
        k0 = pl.multiple_of(kt * tk, tk)
        picked = _tn_dot(sel_t, e_ref[kt])
        ok = (picked > 0.5) & (k0 + lane <= t_col)
        return _flash_step(qm, ks_ref[pl.ds(k0, tk), :], vs_ref[pl.ds(k0, tk), :], rep,
                           tile4(ok), *carry)

    n_kt = (q0 + tq + tk - 1) // tk
    _, l_s, acc_s = lax.fori_loop(0, n_kt, sel_body, init)

    def win_body(kt, carry):
        k0 = pl.multiple_of(kt * tk, tk)
        dlt = t_col - (k0 + lane)
        ok = (dlt >= 0) & (dlt < WINDOW)
        return _flash_step(qm, kw_ref[pl.ds(k0, tk), :], vw_ref[pl.ds(k0, tk), :], rep,
                           tile4(ok), *carry)

    kt_lo = jnp.maximum(q0 - (WINDOW - 1), 0) // tk
    _, l_w, acc_w = lax.fori_loop(kt_lo, n_kt, win_body, init)

    o_s = _merge_heads(acc_s / l_s, tq)
    o_w = _merge_heads(acc_w / l_w, tq)
    gates = g_ref[...]
    g_s = jnp.dot(gates, gs_ref[0, 0], precision=HIGHEST, preferred_element_type=F32)
    g_w = jnp.dot(gates, gw_ref[0, 0], precision=HIGHEST, preferred_element_type=F32)
    o_ref[...] = (oc_ref[...] + g_s * o_s + g_w * o_w).astype(o_ref.dtype)


def nsa_attn(qkv, aux, sel_t, oc, repsel, gsel, e3, bsz, seq, tq=256, tk=256):
    n = qkv.shape[0]
    g = NSA_KV_GROUPS
    nq = seq // tq
    ns = seq // SLC_BLOCK
    nqt = NSA_HEADS * NSA_HEAD_DIM // GW
    row = lambda b, gi, i: b * nq + i
    kv = lambda c: pl.BlockSpec((seq, GW), lambda b, gi, i: (b, nqt + c))
    return pl.pallas_call(
        functools.partial(_attn_kernel, tk),
        grid=(bsz, g, nq),
        in_specs=[
            pl.BlockSpec((tq, GW), lambda b, gi, i: (row(b, gi, i), gi)),
            kv(0), kv(1), kv(2), kv(3),
            pl.BlockSpec((1, GW, GW), lambda b, gi, i: (gi, 0, 0)),
            pl.BlockSpec((1, 1, ns, tq), lambda b, gi, i: (b, gi, 0, i)),
            pl.BlockSpec((seq // tk, ns, tk), lambda b, gi, i: (0, 0, 0)),
            pl.BlockSpec((tq, GW), lambda b, gi, i: (row(b, gi, i), 2)),
            pl.BlockSpec((1, 1, GW, GW), lambda b, gi, i: (1, gi, 0, 0)),
            pl.BlockSpec((1, 1, GW, GW), lambda b, gi, i: (2, gi, 0, 0)),
            pl.BlockSpec((tq, GW), lambda b, gi, i: (row(b, gi, i), gi)),
        ],
        out_specs=pl.BlockSpec((tq, GW), lambda b, gi, i: (row(b, gi, i), gi)),
        out_shape=jax.ShapeDtypeStruct((n, g * GW), BF16),
        compiler_params=_cparams("parallel", "parallel", "parallel"),
        name="nsa_attn",
    )(qkv, qkv, qkv, qkv, qkv, repsel, sel_t, e3, aux, gsel, gsel, oc)


def _out_proj_kernel(alpha, a_ref, x_ref, gm_ref, w_ref, lg_ref, lb_ref, o_ref):
    y = jnp.dot(a_ref[...], w_ref[...], preferred_element_type=F32)
    o_ref[...] = _layer_norm(alpha * x_ref[...] + (1.0 + gm_ref[0]) * y, lg_ref[...], lb_ref[...])


def out_proj_ln(a_bf, x2, mod3, chunk, w_bf, ln_g, ln_b, seq, alpha, tm=512):
    n, d = x2.shape
    k = a_bf.shape[1]
    tpb = seq // tm
    return pl.pallas_call(
        functools.partial(_out_proj_kernel, alpha),
        grid=(n // tm,),
        in_specs=[
            pl.BlockSpec((tm, k), lambda i: (i, 0)),
            pl.BlockSpec((tm, d), lambda i: (i, 0)),
            _mod_spec(d, tpb, chunk, 1),
            pl.BlockSpec((k, d), lambda i: (0, 0)),
            pl.BlockSpec((1, d), lambda i: (0, 0)),
            pl.BlockSpec((1, d), lambda i: (0, 0)),
        ],
        out_specs=pl.BlockSpec((tm, d), lambda i: (i, 0)),
        out_shape=jax.ShapeDtypeStruct((n, d), F32),
        compiler_params=_cparams("parallel"),
        name="out_proj_ln",
    )(a_bf, x2, mod3, w_bf, _row(ln_g), _row(ln_b))


def _nsa_constants(seq, tk):
    g, r4, dh = NSA_KV_GROUPS, NSA_HEADS_PER_GROUP, NSA_HEAD_DIM
    lane = jnp.arange(GW)
    repsel = (lane[None, :, None] == (jnp.arange(g)[:, None, None] * dh + lane[None, None, :] % dh))
    col = (jnp.arange(g)[:, None] * r4 + lane[None, :] // dh) * 3
    gsel = (lane[None, None, :, None] == (col[None, :, None, :] + jnp.arange(3)[:, None, None, None]))
    nc = seq // CMP_STRIDE
    ns = seq // SLC_BLOCK
    sub = SLC_BLOCK // CMP_STRIDE
    seg = jnp.arange(nc)
    c2s_t = sum(((seg[None, :] + o) // sub == jnp.arange(ns)[:, None]).astype(F32)
                for o in range(CMP_BLOCK // CMP_STRIDE))
    key = jnp.arange(seq).reshape(seq // tk, 1, tk)
    e3 = (key // SLC_BLOCK == jnp.arange(ns)[None, :, None])
    return repsel.astype(BF16), gsel.astype(F32), c2s_t, e3.astype(BF16)


def nsa_layer(x2, mod3, positions, w_in, cmp_pos, cmp_w1, cmp_w2, w_out, ln_g, ln_b, seq, alpha,
              tq=256, tk=256):
    n, d = x2.shape
    bsz = n // seq
    g, dh = NSA_KV_GROUPS, NSA_HEAD_DIM
    qw, kvw = NSA_HEADS * dh, g * dh
    cols = lambda k: w_in[:, qw + k * kvw: qw + (k + 1) * kvw]
    gl = w_in[:, qw + 6 * kvw:]
    w_main = jnp.concatenate([w_in[:, :qw], cols(2), cols(3), cols(4), cols(5)], axis=1).astype(BF16)
    w_aux = jnp.concatenate([cols(0), cols(1), jnp.pad(gl, ((0, 0), (0, GW - gl.shape[1])))],
                            axis=1).astype(BF16)
    tables = _rope_tables(positions)
    nqt = qw // GW
    kinds_main = (ROPE,) * nqt + (ROPE, PLAIN, ROPE, PLAIN)
    qkv = nsa_proj(x2, mod3, w_main, tables, kinds_main, BF16, seq, "nsa_proj_main")
    aux = nsa_proj(x2, mod3, w_aux, tables, (PLAIN, PLAIN, SIGMOID), F32, seq, "nsa_proj_aux")

    hid = cmp_w1.shape[-1]
    eye = jnp.eye(g, dtype=F32)
    w1bd = jnp.einsum('spdh,gk->spgdkh', cmp_w1.reshape(2, CMP_BLOCK, dh, hid), eye)
    w1bd = w1bd.reshape(2, CMP_BLOCK, g * dh, g * hid).astype(BF16)
    w2bd = jnp.einsum('shd,gk->sghkd', cmp_w2, eye).reshape(2, g * hid, g * dh).astype(BF16)
    pos_t = jnp.tile(cmp_pos, (1, 1, g))
    nh = seq // CMP_STRIDE
    cmp_positions = jnp.pad(positions[:, CMP_BLOCK - 1::CMP_STRIDE], ((0, 0), (0, 1)))[:, :nh]
    ctabs = tuple(t.reshape(bsz, nh, LANES) for t in _rope_tables(cmp_positions))
    kvcmp = nsa_compress(aux, w1bd, w2bd, pos_t[:, :CMP_STRIDE], pos_t[:, CMP_STRIDE:], ctabs, bsz, seq)

    repsel, gsel, c2s_t, e3 = _nsa_constants(seq, tk)
    oc, sel_t = nsa_cmp_sel(qkv, kvcmp, aux, repsel, gsel, c2s_t, bsz, seq, tq)
    o = nsa_attn(qkv, aux, sel_t, oc, repsel, gsel, e3, bsz, seq, tq, tk)
    return out_proj_ln(o, x2, mod3, 2, w_out.astype(BF16), ln_g, ln_b, seq, alpha)


MOE_ROWS = 512
RINFO = 8


def _router_kernel(x_ref, sh_ref, sc_ref, r_ref, tril_ref, info_ref, cnt_ref, carry_ref):
    tm = x_ref.shape[0]
    i = pl.program_id(0)

    @pl.when(i == 0)
    def _():
        carry_ref[...] = jnp.zeros_like(carry_ref)

    u = x_ref[...] * (1.0 + sc_ref[0]) + sh_ref[0]
    logits = jnp.dot(u, r_ref[...], precision=HIGHEST, preferred_element_type=F32)
    lane = lax.broadcasted_iota(jnp.int32, (1, LANES), 1)
    logits = jnp.where(lane < N_EXPERTS, logits, -jnp.inf)
    m1 = jnp.max(logits, axis=-1, keepdims=True)
    e1 = jnp.min(jnp.where(logits == m1, lane, LANES), axis=-1, keepdims=True)
    rest = jnp.where(lane == e1, -jnp.inf, logits)
    m2 = jnp.max(rest, axis=-1, keepdims=True)
    e2 = jnp.min(jnp.where(rest == m2, lane, LANES), axis=-1, keepdims=True)
    ex = jnp.exp(m2 - m1)
    g1 = 1.0 / (1.0 + ex)
    g2 = ex / (1.0 + ex)
    o1 = lane == e1
    o2 = lane == e2
    tot = jnp.where(o1 | o2, 1.0, 0.0)
    before = carry_ref[...] + jnp.dot(tril_ref[...], tot.astype(BF16), preferred_element_type=F32)
    rank1 = jnp.sum(jnp.where(o1, before, 0.0), axis=-1, keepdims=True)
    rank2 = jnp.sum(jnp.where(o2, before, 0.0), axis=-1, keepdims=True)
    carry_ref[...] += jnp.sum(tot, axis=0, keepdims=True)
    cnt_ref[...] = carry_ref[...]
    li = lax.broadcasted_iota(jnp.int32, (1, RINFO), 1)
    info = jnp.zeros((tm, RINFO), F32)
    for k, v in enumerate((e1.astype(F32), e2.astype(F32), g1, g2, rank1, rank2)):
        info = jnp.where(li == k, v, info)
    info_ref[...] = info


def moe_router(x2, mod3, router, seq, tm=512):
    n, d = x2.shape
    tpb = seq // tm
    r_pad = jnp.pad(router, ((0, 0), (0, LANES - router.shape[1])))
    tril = (jnp.arange(tm)[:, None] > jnp.arange(tm)[None, :]).astype(BF16)
    return pl.pallas_call(
        _router_kernel,
        grid=(n // tm,),
        in_specs=[
            pl.BlockSpec((tm, d), lambda i: (i, 0)),
            _mod_spec(d, tpb, 3, 1), _mod_spec(d, tpb, 4, 1),
            pl.BlockSpec((d, LANES), lambda i: (0, 0)),
            pl.BlockSpec((tm, tm), lambda i: (0, 0)),
        ],
        out_specs=[
            pl.BlockSpec((tm, RINFO), lambda i: (i, 0)),
            pl.BlockSpec((1, LANES), lambda i: (0, 0)),
        ],
        out_shape=[
            jax.ShapeDtypeStruct((n, RINFO), F32),
            jax.ShapeDtypeStruct((1, LANES), F32),
        ],
        scratch_shapes=[pltpu.VMEM((1, LANES), F32)],
        compiler_params=_cparams("arbitrary"),
        name="moe_router",
    )(x2, mod3, mod3, r_pad, tril)


def _dispatch_kernel(dest_ref, x_ref, sh_ref, sc_ref, xs_in_ref, xs_ref, u_ref, sem):
    del xs_in_ref
    tm = x_ref.shape[0]
    u_ref[...] = x_ref[...] * (1.0 + sc_ref[0]) + sh_ref[0]

    def row_copy(r, k):
        return pltpu.make_async_copy(u_ref.at[pl.ds(r, 1)], xs_ref.at[pl.ds(dest_ref[TOP_K * r + k], 1)], sem)

    def issue(r, carry):
        for k in range(TOP_K):
            row_copy(r, k).start()
        return carry

    def drain(r, carry):
        for k in range(TOP_K):
            row_copy(r, k).wait()
        return carry

    lax.fori_loop(0, tm, issue, 0)
    lax.fori_loop(0, tm, drain, 0)


def moe_dispatch(x2, mod3, dest, cap, seq, tm=256):
    n, d = x2.shape
    tpb = seq // tm
    xs0 = jnp.zeros((cap, d), F32)
    return pl.pallas_call(
        _dispatch_kernel,
        grid=(n // tm,),
        in_specs=[
            pl.BlockSpec((TOP_K * tm,), lambda i: (i,), memory_space=pltpu.SMEM),
            pl.BlockSpec((tm, d), lambda i: (i, 0)),
            _mod_spec(d, tpb, 3, 1), _mod_spec(d, tpb, 4, 1),
            pl.BlockSpec(memory_space=pl.ANY),
        ],
        out_specs=pl.BlockSpec(memory_space=pl.ANY),
        out_shape=jax.ShapeDtypeStruct((cap, d), F32),
        scratch_shapes=[pltpu.VMEM((tm, d), F32), pltpu.SemaphoreType.DMA],
        input_output_aliases={4: 0},
        compiler_params=_cparams("arbitrary"),
        name="moe_dispatch",
    )(dest, x2, mod3, mod3, xs0)


def _moe_ffn_kernel(bexp_ref, nused_ref, xs_ref, w1_ref, w3_ref, w2_ref, o_ref, xb_ref, acc_ref):
    del bexp_ref
    f = pl.program_id(1)
    used = pl.program_id(0) < nused_ref[0]

    @pl.when(used & (f == 0))
    def _():
        xb_ref[...] = xs_ref[...].astype(BF16)

    @pl.when(used)
    def _():
        xb = xb_ref[...]
        h = _silu(jnp.dot(xb, w1_ref[0], preferred_element_type=F32)) * jnp.dot(
            xb, w3_ref[0], preferred_element_type=F32)
        part = jnp.dot(h.astype(BF16), w2_ref[0], preferred_element_type=F32)

        @pl.when(f == 0)
        def _():
            acc_ref[...] = part

        @pl.when(f > 0)
        def _():
            acc_ref[...] += part

    @pl.when(f == pl.num_programs(1) - 1)
    def _():
        o_ref[...] = jnp.where(used, acc_ref[...], 0.0)


def moe_ffn(xs, blk_exp, n_used, w1_bf, w3_bf, w2_bf, tf=512):
    cap, d = xs.shape
    ff = w1_bf.shape[2]
    grid_spec = pltpu.PrefetchScalarGridSpec(
        num_scalar_prefetch=2,
        grid=(cap // MOE_ROWS, ff // tf),
        in_specs=[
            pl.BlockSpec((MOE_ROWS, d), lambda i, f, be, nu: (i, 0)),
            pl.BlockSpec((1, d, tf), lambda i, f, be, nu: (be[i], 0, f)),
            pl.BlockSpec((1, d, tf), lambda i, f, be, nu: (be[i], 0, f)),
            pl.BlockSpec((1, tf, d), lambda i, f, be, nu: (be[i], f, 0)),
        ],
        out_specs=pl.BlockSpec((MOE_ROWS, d), lambda i, f, be, nu: (i, 0)),
        scratch_shapes=[pltpu.VMEM((MOE_ROWS, d), BF16), pltpu.VMEM((MOE_ROWS, d), F32)],
    )
    return pl.pallas_call(
        _moe_ffn_kernel,
        grid_spec=grid_spec,
        out_shape=jax.ShapeDtypeStruct((cap, d), F32),
        compiler_params=_cparams("parallel", "arbitrary"),
        name="moe_ffn",
    )(blk_exp, n_used, xs, w1_bf, w3_bf, w2_bf)


def _combine_kernel(alpha, dest_ref, info_ref, x_ref, gf_ref, ys_ref, lg_ref, lb_ref, o_ref, buf_ref, sem):
    tm = x_ref.shape[0]

    def row_copy(r, k):
        return pltpu.make_async_copy(ys_ref.at[pl.ds(dest_ref[TOP_K * r + k], 1)],
                                     buf_ref.at[k, pl.ds(r, 1)], sem)

    def issue(r, carry):
        for k in range(TOP_K):
            row_copy(r, k).start()
        return carry

    def drain(r, carry):
        for k in range(TOP_K):
            row_copy(r, k).wait()
        return carry

    lax.fori_loop(0, tm, issue, 0)
    lax.fori_loop(0, tm, drain, 0)
    info = info_ref[...]
    y = info[:, 2:3] * buf_ref[0] + info[:, 3:4] * buf_ref[1]
    o_ref[...] = _layer_norm(alpha * x_ref[...] + (1.0 + gf_ref[0]) * y, lg_ref[...], lb_ref[...])


def moe_combine(ys, dest, info, x2, mod3, ln_g, ln_b, seq, alpha, tm=256):
    n, d = x2.shape
    tpb = seq // tm
    return pl.pallas_call(
        functools.partial(_combine_kernel, alpha),
        grid=(n // tm,),
        in_specs=[
            pl.BlockSpec((TOP_K * tm,), lambda i: (i,), memory_space=pltpu.SMEM),
            pl.BlockSpec((tm, RINFO), lambda i: (i, 0)),
            pl.BlockSpec((tm, d), lambda i: (i, 0)),
            _mod_spec(d, tpb, 5, 1),
            pl.BlockSpec(memory_space=pl.ANY),
            pl.BlockSpec((1, d), lambda i: (0, 0)),
            pl.BlockSpec((1, d), lambda i: (0, 0)),
        ],
        out_specs=pl.BlockSpec((tm, d), lambda i: (i, 0)),
        out_shape=jax.ShapeDtypeStruct((n, d), F32),
        scratch_shapes=[pltpu.VMEM((TOP_K, tm, d), F32), pltpu.SemaphoreType.DMA],
        compiler_params=_cparams("arbitrary"),
        name="moe_combine",
    )(dest, info, x2, mod3, ys, _row(ln_g), _row(ln_b))


def moe_layer(x2, mod3, router, w1, w3, w2, ln_g, ln_b, seq, alpha):
    n, d = x2.shape
    info, counts = moe_router(x2, mod3, router, seq)
    sizes = counts[0, :N_EXPERTS].astype(jnp.int32)
    padded = (sizes + MOE_ROWS - 1) // MOE_ROWS * MOE_ROWS
    pend = jnp.cumsum(padded)
    pstart = pend - padded
    cap = n * TOP_K + N_EXPERTS * MOE_ROWS
    n_blk = cap // MOE_ROWS
    blk_lo = jnp.arange(n_blk, dtype=jnp.int32) * MOE_ROWS
    blk_exp = jnp.minimum(jnp.sum(blk_lo[:, None] >= pend[None, :], axis=1), N_EXPERTS - 1).astype(jnp.int32)
    n_used = (pend[-1:] // MOE_ROWS).astype(jnp.int32)
    ids = jnp.arange(N_EXPERTS, dtype=jnp.int32)
    e = info[:, 0:TOP_K].astype(jnp.int32)
    start = jnp.sum(jnp.where(e[:, :, None] == ids, pstart, 0), axis=-1)
    dest = (start + info[:, 4:4 + TOP_K].astype(jnp.int32)).reshape(-1)
    xs = moe_dispatch(x2, mod3, dest, cap, seq)
    ys = moe_ffn(xs, blk_exp, n_used, w1.astype(BF16), w3.astype(BF16), w2.astype(BF16))
    return moe_combine(ys, dest, info, x2, mod3, ln_g, ln_b, seq, alpha)


def kernel(x, c, positions, ada_w, ada_b, ln_g, ln_b, conv_w_in, conv_b_in, conv_dw, conv_dw_b, conv_ln_g, conv_ln_b, conv_w_out, conv_b_out, nsa_w_in, nsa_cmp_pos, nsa_cmp_w1, nsa_cmp_w2, nsa_w_out, pool_w, pool_scale, ffn_w1, ffn_w3, ffn_w2, moe_router, moe_w1, moe_w3, moe_w2):
    bsz, seq, d = x.shape
    depth = ada_w.shape[0]
    alpha = (2 * depth) ** 0.25
    mod = ada_mod(c, ada_w, ada_b)
    x2 = x.reshape(bsz * seq, d)
    mods = mod.reshape(depth, bsz, 1, 6 * d)
    for i in range(depth):
        mod3 = mods[i]
        j = i // N_MIXERS
        if i % N_MIXERS == 0:
            h = conv_in(x2, mod3, conv_w_in[j].astype(BF16), conv_b_in[j], seq)
            x2 = conv_tail(h, x2, mod3, conv_dw[j], conv_dw_b[j], conv_ln_g[j], conv_ln_b[j],
                           conv_w_out[j].astype(BF16), conv_b_out[j], ln_g[i, 0], ln_b[i, 0], seq, alpha)
        elif i % N_MIXERS == 1:
            x2 = nsa_layer(x2, mod3, positions, nsa_w_in[j], nsa_cmp_pos[j], nsa_cmp_w1[j], nsa_cmp_w2[j],
                           nsa_w_out[j], ln_g[i, 0], ln_b[i, 0], seq, alpha)
        else:
            x2 = pool_layer(x2, mod3, pool_w[j].astype(BF16), pool_scale[j], ln_g[i, 0], ln_b[i, 0],
                            seq, alpha)
        j = i // 2
        if i % 2 == 0:
            x2 = ffn_dense(x2, mod3, ffn_w1[j].astype(BF16), ffn_w3[j].astype(BF16),
                           ffn_w2[j].astype(BF16), ln_g[i, 1], ln_b[i, 1], seq, alpha)
        else:
            x2 = moe_layer(x2, mod3, moe_router[j], moe_w1[j], moe_w3[j], moe_w2[j],
                           ln_g[i, 1], ln_b[i, 1], seq, alpha)
    return x2.reshape(bsz, seq, d)
```

```python
import functools

import jax
import jax.numpy as jnp
from jax import lax
from jax.experimental import pallas as pl
from jax.experimental.pallas import tpu as pltpu

F32 = jnp.float32
BF16 = jnp.bfloat16
HIGHEST = lax.Precision.HIGHEST

N_MIXERS = 3
CONV_WIDTH = 31
NSA_HEADS = 16
NSA_KV_GROUPS = 4
NSA_HEADS_PER_GROUP = NSA_HEADS // NSA_KV_GROUPS
NSA_HEAD_DIM = 64
CMP_BLOCK = 32
CMP_STRIDE = 16
SLC_BLOCK = 64
SLC_TOP_N = 16
WINDOW = 512
ROPE_THETA = 500000.0
ROT_DIM = NSA_HEAD_DIM // 4
POOL_WINDOWS = (2, 4, 8, 16)
N_EXPERTS = 8
TOP_K = 2
LN_EPS = 1e-5
NEG_INF = -1e30
SEL_BIG = 1e9

LANES = 128
SUBLANES = 8
VMEM_LIMIT = 56 * 1024 * 1024


def _cparams(*sem):
    return pltpu.CompilerParams(dimension_semantics=sem, vmem_limit_bytes=VMEM_LIMIT)


def _layer_norm(v, g, b):
    mu = jnp.mean(v, axis=-1, keepdims=True)
    d = v - mu
    var = jnp.mean(d * d, axis=-1, keepdims=True)
    return d * lax.rsqrt(var + LN_EPS) * g + b


def _silu(v):
    return v * jax.nn.sigmoid(v)


def _row(v):
    return v.reshape(1, -1)


def _ada_kernel(c_ref, w_ref, b_ref, o_ref):
    cond = _silu(c_ref[...])
    o_ref[0] = jnp.dot(cond, w_ref[0], precision=HIGHEST, preferred_element_type=F32) + b_ref[0]


def ada_mod(c, ada_w, ada_b):
    depth, d, d6 = ada_w.shape
    b = c.shape[0]
    tn = d6 // 4
    return pl.pallas_call(
        _ada_kernel,
        grid=(depth, d6 // tn),
        in_specs=[
            pl.BlockSpec((b, d), lambda i, j: (0, 0)),
            pl.BlockSpec((1, d, tn), lambda i, j: (i, 0, j)),
            pl.BlockSpec((1, 1, tn), lambda i, j: (i, 0, j)),
        ],
        out_specs=pl.BlockSpec((1, b, tn), lambda i, j: (i, 0, j)),
        out_shape=jax.ShapeDtypeStruct((depth, b, d6), F32),
        compiler_params=_cparams("parallel", "parallel"),
        name="ada_mod",
    )(c, ada_w, ada_b.reshape(depth, 1, d6))


def _mod_spec(d, tiles_per_batch, chunk, ngrid):
    if ngrid == 1:
        return pl.BlockSpec((1, 1, d), lambda i: (i // tiles_per_batch, 0, chunk))
    return pl.BlockSpec((1, 1, d), lambda i, j: (i // tiles_per_batch, 0, chunk))


def _conv_in_kernel(x_ref, sh_ref, sc_ref, wa_ref, wg_ref, ba_ref, bg_ref, h_ref, u_ref):
    @pl.when(pl.program_id(1) == 0)
    def _():
        u_ref[...] = (x_ref[...] * (1.0 + sc_ref[0]) + sh_ref[0]).astype(BF16)

    u = u_ref[...]
    a = jnp.dot(u, wa_ref[...], preferred_element_type=F32) + ba_ref[...]
    g = jnp.dot(u, wg_ref[...], preferred_element_type=F32) + bg_ref[...]
    h_ref[...] = a * jax.nn.sigmoid(g)


def conv_in(x2, mod3, w_in_bf, b_in, seq, tm=512, tn=512):
    n, d = x2.shape
    tpb = seq // tm
    nj = d // tn
    b2 = _row(b_in)
    return pl.pallas_call(
        _conv_in_kernel,
        grid=(n // tm, nj),
        in_specs=[
            pl.BlockSpec((tm, d), lambda i, j: (i, 0)),
            _mod_spec(d, tpb, 0, 2),
            _mod_spec(d, tpb, 1, 2),
            pl.BlockSpec((d, tn), lambda i, j: (0, j)),
            pl.BlockSpec((d, tn), lambda i, j: (0, j + nj)),
            pl.BlockSpec((1, tn), lambda i, j: (0, j)),
            pl.BlockSpec((1, tn), lambda i, j: (0, j + nj)),
        ],
        out_specs=pl.BlockSpec((tm, tn), lambda i, j: (i, j)),
        out_shape=jax.ShapeDtypeStruct((n, d), F32),
        scratch_shapes=[pltpu.VMEM((tm, d), BF16)],
        compiler_params=_cparams("parallel", "arbitrary"),
        name="conv_in",
    )(x2, mod3, mod3, w_in_bf, w_in_bf, b2, b2)


CONV_HALO = 32
CONV_ROWS = 16


def _conv_tail_kernel(tpb, alpha, hc_ref, hp_ref, x_ref, gm_ref, dw_ref, dwb_ref, cg_ref, cb_ref,
                      wo_ref, bo_ref, lg_ref, lb_ref, o_ref, win_ref, cv_ref):
    tm = hc_ref.shape[0]
    first = (pl.program_id(0) % tpb) == 0
    win_ref[0, 0:CONV_HALO, :] = jnp.where(first, 0.0, hp_ref[...])
    win_ref[0, CONV_HALO:CONV_HALO + tm, :] = hc_ref[...]
    keep = tm + CONV_HALO - SUBLANES
    for s in range(1, SUBLANES):
        win_ref[s, 0:keep, :] = win_ref[0, s:s + keep, :]
    off = CONV_HALO - (CONV_WIDTH - 1)
    halves = CONV_ROWS // SUBLANES
    for rc in range(tm // CONV_ROWS):
        r0 = rc * CONV_ROWS
        accs = [None] * halves
        for k in range(CONV_WIDTH):
            j = off + k
            a0 = r0 + j - j % SUBLANES
            tap = dw_ref[k]
            for i in range(halves):
                term = win_ref[j % SUBLANES, a0 + i * SUBLANES:a0 + (i + 1) * SUBLANES, :] * tap
                accs[i] = term + dwb_ref[...] if accs[i] is None else accs[i] + term
        for i in range(halves):
            cv_ref[r0 + i * SUBLANES:r0 + (i + 1) * SUBLANES, :] = accs[i]
    hn = _silu(_layer_norm(cv_ref[...], cg_ref[...], cb_ref[...]))
    y = jnp.dot(hn.astype(BF16), wo_ref[...], preferred_element_type=F32) + bo_ref[...]
    o_ref[...] = _layer_norm(alpha * x_ref[...] + (1.0 + gm_ref[0]) * y, lg_ref[...], lb_ref[...])


def conv_tail(h, x2, mod3, dw, dw_b, cln_g, cln_b, w_out_bf, b_out, ln_g, ln_b, seq, alpha, tm=256):
    n, d = x2.shape
    tpb = seq // tm
    hb = tm // CONV_HALO
    full = lambda shape: pl.BlockSpec(shape, lambda i: (0, 0))
    return pl.pallas_call(
        functools.partial(_conv_tail_kernel, tpb, alpha),
        grid=(n // tm,),
        in_specs=[
            pl.BlockSpec((tm, d), lambda i: (i, 0)),
            pl.BlockSpec((CONV_HALO, d), lambda i: (jnp.maximum(i * hb - 1, 0), 0)),
            pl.BlockSpec((tm, d), lambda i: (i, 0)),
            _mod_spec(d, tpb, 2, 1),
            pl.BlockSpec((CONV_WIDTH, SUBLANES, d), lambda i: (0, 0, 0)), full((1, d)), full((1, d)), full((1, d)),
            full((d, d)), full((1, d)), full((1, d)), full((1, d)),
        ],
        out_specs=pl.BlockSpec((tm, d), lambda i: (i, 0)),
        out_shape=jax.ShapeDtypeStruct((n, d), F32),
        scratch_shapes=[pltpu.VMEM((SUBLANES, tm + CONV_HALO, d), F32), pltpu.VMEM((tm, d), F32)],
        compiler_params=_cparams("parallel"),
        name="conv_tail",
    )(h, h, x2, mod3, jnp.broadcast_to(dw[:, None, :], (CONV_WIDTH, SUBLANES, d)), _row(dw_b), _row(cln_g),
      _row(cln_b), w_out_bf, _row(b_out), _row(ln_g), _row(ln_b))


def _ffn_kernel(alpha, x_ref, sh_ref, sc_ref, gf_ref, w1_ref, w3_ref, w2_ref, lg_ref, lb_ref,
                o_ref, u_ref, acc_ref):
    f = pl.program_id(1)

    @pl.when(f == 0)
    def _():
        u_ref[...] = (x_ref[...] * (1.0 + sc_ref[0]) + sh_ref[0]).astype(BF16)

    u = u_ref[...]
    h = _silu(jnp.dot(u, w1_ref[...], preferred_element_type=F32)) * jnp.dot(
        u, w3_ref[...], preferred_element_type=F32)
    part = jnp.dot(h.astype(BF16), w2_ref[...], preferred_element_type=F32)

    @pl.when(f == 0)
    def _():
        acc_ref[...] = part

    @pl.when(f > 0)
    def _():
        acc_ref[...] += part

    @pl.when(f == pl.num_programs(1) - 1)
    def _():
        o_ref[...] = _layer_norm(alpha * x_ref[...] + (1.0 + gf_ref[0]) * acc_ref[...],
                                 lg_ref[...], lb_ref[...])


def ffn_dense(x2, mod3, w1_bf, w3_bf, w2_bf, ln_g, ln_b, seq, alpha, tm=1024, tf=512):
    n, d = x2.shape
    ff = w1_bf.shape[1]
    tpb = seq // tm
    return pl.pallas_call(
        functools.partial(_ffn_kernel, alpha),
        grid=(n // tm, ff // tf),
        in_specs=[
            pl.BlockSpec((tm, d), lambda i, f: (i, 0)),
            _mod_spec(d, tpb, 3, 2), _mod_spec(d, tpb, 4, 2), _mod_spec(d, tpb, 5, 2),
            pl.BlockSpec((d, tf), lambda i, f: (0, f)),
            pl.BlockSpec((d, tf), lambda i, f: (0, f)),
            pl.BlockSpec((tf, d), lambda i, f: (f, 0)),
            pl.BlockSpec((1, d), lambda i, f: (0, 0)),
            pl.BlockSpec((1, d), lambda i, f: (0, 0)),
        ],
        out_specs=pl.BlockSpec((tm, d), lambda i, f: (i, 0)),
        out_shape=jax.ShapeDtypeStruct((n, d), F32),
        scratch_shapes=[pltpu.VMEM((tm, d), BF16), pltpu.VMEM((tm, d), F32)],
        compiler_params=_cparams("parallel", "arbitrary"),
        name="ffn_dense",
    )(x2, mod3, mod3, mod3, w1_bf, w3_bf, w2_bf, _row(ln_g), _row(ln_b))


POOL_HALO = 16


def _pool_kernel(tpb, alpha, xc_ref, xp_ref, sh_ref, sc_ref, gm_ref, w_ref, ps_ref, lg_ref, lb_ref,
                 o_ref, win_ref):
    tm, d = xc_ref.shape
    gw = d // len(POOL_WINDOWS)
    tile = pl.program_id(0) % tpb
    first = tile == 0
    scale = 1.0 + sc_ref[0]
    up = xp_ref[...] * scale + sh_ref[0]
    uc = xc_ref[...] * scale + sh_ref[0]
    win_ref[0:POOL_HALO, :] = jnp.where(first, 0.0, up)
    win_ref[POOL_HALO:POOL_HALO + tm, :] = uc
    t = tile * tm + lax.broadcasted_iota(jnp.int32, (tm, 1), 0)
    ys = []
    for gi, w in enumerate(POOL_WINDOWS):
        c0 = gi * gw
        s = uc[:, c0:c0 + gw]
        for k in range(1, w):
            s = s + win_ref[POOL_HALO - k:POOL_HALO - k + tm, c0:c0 + gw]
        cnt = jnp.minimum(t + 1, w).astype(F32)
        p = s / cnt - uc[:, c0:c0 + gw]
        ys.append(jnp.dot(p.astype(BF16), w_ref[gi], preferred_element_type=F32))
    y = jnp.concatenate(ys, axis=-1) * ps_ref[...]
    o_ref[...] = _layer_norm(alpha * xc_ref[...] + (1.0 + gm_ref[0]) * y, lg_ref[...], lb_ref[...])


def pool_layer(x2, mod3, w_grp_bf, pool_scale, ln_g, ln_b, seq, alpha, tm=512):
    n, d = x2.shape
    tpb = seq // tm
    hb = tm // POOL_HALO
    ng, gw, _ = w_grp_bf.shape
    return pl.pallas_call(
        functools.partial(_pool_kernel, tpb, alpha),
        grid=(n // tm,),
        in_specs=[
            pl.BlockSpec((tm, d), lambda i: (i, 0)),
            pl.BlockSpec((POOL_HALO, d), lambda i: (jnp.maximum(i * hb - 1, 0), 0)),
            _mod_spec(d, tpb, 0, 1), _mod_spec(d, tpb, 1, 1), _mod_spec(d, tpb, 2, 1),
            pl.BlockSpec((ng, gw, gw), lambda i: (0, 0, 0)),
            pl.BlockSpec((1, d), lambda i: (0, 0)),
            pl.BlockSpec((1, d), lambda i: (0, 0)),
            pl.BlockSpec((1, d), lambda i: (0, 0)),
        ],
        out_specs=pl.BlockSpec((tm, d), lambda i: (i, 0)),
        out_shape=jax.ShapeDtypeStruct((n, d), F32),
        scratch_shapes=[pltpu.VMEM((tm + POOL_HALO, d), F32)],
        compiler_params=_cparams("parallel"),
        name="pool_layer",
    )(x2, x2, mod3, mod3, mod3, w_grp_bf, _row(pool_scale), _row(ln_g), _row(ln_b))


GW = NSA_HEADS_PER_GROUP * NSA_HEAD_DIM
PLAIN, ROPE, SIGMOID = 0, 1, 2


def _rope_tables(positions):
    half = ROT_DIM // 2
    inv = ROPE_THETA ** (-jnp.arange(half, dtype=F32) * 2.0 / ROT_DIM)
    ang = positions.astype(F32)[..., None] * inv
    cos, sin = jnp.cos(ang), jnp.sin(ang)
    l64 = jnp.arange(LANES) % NSA_HEAD_DIM
    idx = l64 % half
    lo, hi = l64 < half, (l64 >= half) & (l64 < ROT_DIM)
    ct = jnp.where(l64 < ROT_DIM, cos[..., idx], 1.0)
    sp = jnp.where(hi, sin[..., idx], 0.0)
    sm = jnp.where(lo, -sin[..., idx], 0.0)
    flat = lambda a: a.reshape(-1, LANES)
    return flat(ct), flat(sp), flat(sm)


def _rope(v, ct, sp, sm):
    half = ROT_DIM // 2
    outs = []
    for c0 in range(0, v.shape[1], LANES):
        vc = v[:, c0:c0 + LANES]
        outs.append(vc * ct + pltpu.roll(vc, half, 1) * sp + pltpu.roll(vc, LANES - half, 1) * sm)
    return outs[0] if len(outs) == 1 else jnp.concatenate(outs, axis=1)


def _proj_kernel(kinds, x_ref, sh_ref, sc_ref, w_ref, ct_ref, sp_ref, sm_ref, o_ref, u_ref):
    j = pl.program_id(1)

    @pl.when(j == 0)
    def _():
        u_ref[...] = (x_ref[...] * (1.0 + sc_ref[0]) + sh_ref[0]).astype(BF16)

    acc = jnp.dot(u_ref[...], w_ref[...], preferred_element_type=F32)

    def is_kind(kind):
        hit = False
        for jj, kk in enumerate(kinds):
            if kk == kind:
                hit = (j == jj) | hit
        return hit

    if ROPE in kinds:
        @pl.when(is_kind(ROPE))
        def _():
            o_ref[...] = _rope(acc, ct_ref[...], sp_ref[...], sm_ref[...]).astype(o_ref.dtype)
    if SIGMOID in kinds:
        @pl.when(is_kind(SIGMOID))
        def _():
            o_ref[...] = jax.nn.sigmoid(acc).astype(o_ref.dtype)
    if PLAIN in kinds:
        @pl.when(is_kind(PLAIN))
        def _():
            o_ref[...] = acc.astype(o_ref.dtype)


def nsa_proj(x2, mod3, w_bf, tables, kinds, out_dtype, seq, name, tm=512):
    n, d = x2.shape
    tn = GW
    tpb = seq // tm
    assert w_bf.shape[1] == tn * len(kinds)
    tab = pl.BlockSpec((tm, LANES), lambda i, j: (i, 0))
    return pl.pallas_call(
        functools.partial(_proj_kernel, kinds),
        grid=(n // tm, len(kinds)),
        in_specs=[
            pl.BlockSpec((tm, d), lambda i, j: (i, 0)),
            _mod_spec(d, tpb, 0, 2), _mod_spec(d, tpb, 1, 2),
            pl.BlockSpec((d, tn), lambda i, j: (0, j)),
            tab, tab, tab,
        ],
        out_specs=pl.BlockSpec((tm, tn), lambda i, j: (i, j)),
        out_shape=jax.ShapeDtypeStruct((n, w_bf.shape[1]), out_dtype),
        scratch_shapes=[pltpu.VMEM((tm, d), BF16)],
        compiler_params=_cparams("parallel", "arbitrary"),
        name=name,
    )(x2, mod3, mod3, w_bf, *tables)


def _gelu_tanh(v):
    return 0.5 * v * (1.0 + jnp.tanh(0.7978845608028654 * (v + 0.044715 * (v * v * v))))


def _compress_kernel(xl_ref, xh_ref, w1_ref, w2_ref, pa_ref, pb_ref, ct_ref, sp_ref, sm_ref, o_ref):
    nh = o_ref.shape[2]
    a = None
    bm = None
    for p in range(CMP_STRIDE):
        hp = jnp.concatenate([xl_ref[pl.ds(p, nh, stride=CMP_STRIDE), :],
                              xh_ref[pl.ds(p, nh, stride=CMP_STRIDE), :]], axis=1)
        da = jnp.dot((hp + pa_ref[0, p:p + 1, :]).astype(BF16), w1_ref[0, p], preferred_element_type=F32)
        db = jnp.dot((hp + pb_ref[0, p:p + 1, :]).astype(BF16), w1_ref[0, CMP_STRIDE + p],
                     preferred_element_type=F32)
        a = da if a is None else a + da
        bm = db if bm is None else bm + db
    hid = _gelu_tanh(a + pltpu.roll(bm, nh - 1, 0))
    out = jnp.dot(hid.astype(BF16), w2_ref[0], preferred_element_type=F32)

    @pl.when(pl.program_id(1) == 0)
    def _():
        o_ref[0, 0] = _rope(out, ct_ref[0], sp_ref[0], sm_ref[0]).astype(o_ref.dtype)

    @pl.when(pl.program_id(1) != 0)
    def _():
        o_ref[0, 0] = out.astype(o_ref.dtype)


def nsa_compress(aux, w1bd, w2bd, pos_a, pos_b, ctabs, bsz, seq):
    nh = seq // CMP_STRIDE
    hid = w1bd.shape[-1]
    ctab = pl.BlockSpec((1, nh, LANES), lambda b, s: (b, 0, 0))
    return pl.pallas_call(
        _compress_kernel,
        grid=(bsz, 2),
        in_specs=[
            pl.BlockSpec((seq, LANES), lambda b, s: (b, 2 * s)),
            pl.BlockSpec((seq, LANES), lambda b, s: (b, 2 * s + 1)),
            pl.BlockSpec((1, CMP_BLOCK, GW, hid), lambda b, s: (s, 0, 0, 0)),
            pl.BlockSpec((1, hid, GW), lambda b, s: (s, 0, 0)),
            pl.BlockSpec((1, CMP_STRIDE, GW), lambda b, s: (s, 0, 0)),
            pl.BlockSpec((1, CMP_STRIDE, GW), lambda b, s: (s, 0, 0)),
            ctab, ctab, ctab,
        ],
        out_specs=pl.BlockSpec((1, 1, nh, GW), lambda b, s: (s, b, 0, 0)),
        out_shape=jax.ShapeDtypeStruct((2, bsz, nh, GW), BF16),
        compiler_params=_cparams("parallel", "arbitrary"),
        name="nsa_compress",
    )(aux, aux, w1bd, w2bd, pos_a, pos_b, *ctabs)


def _cmp_sel_kernel(q_ref, kc_ref, vct_ref, g_ref, gsel_ref, c2s_ref, oc_ref, sel_ref, qz_ref):
    tq = q_ref.shape[0]
    nc = kc_ref.shape[2]
    r4 = NSA_HEADS_PER_GROUP
    dh = NSA_HEAD_DIM
    g0 = pl.multiple_of(pl.program_id(1) * dh, dh)
    q0 = pl.program_id(2) * tq
    scale = dh ** -0.5
    q_t = (q_ref[...].astype(F32) * scale).T.astype(BF16)
    qz_ref[...] = jnp.zeros(qz_ref.shape, BF16)
    for h in range(r4):
        qz_ref[pl.ds(g0, dh), h * tq:(h + 1) * tq] = q_t[h * dh:(h + 1) * dh, :]
    kc = kc_ref[0, 0]
    vct = vct_ref[0, pl.ds(g0, dh), :]
    t_lanes = q0 + lax.broadcasted_iota(jnp.int32, (1, tq), 1)
    cend = lax.broadcasted_iota(jnp.int32, (nc, 1), 0) * CMP_STRIDE + (CMP_BLOCK - 1)
    ok = cend <= t_lanes
    scores = [jnp.dot(kc, qz_ref[:, h * tq:(h + 1) * tq], preferred_element_type=F32) for h in range(r4)]
    probs = []
    psum = None
    for st in scores:
        st = jnp.where(ok, st, NEG_INF)
        et = jnp.exp(st - jnp.max(st, axis=0, keepdims=True))
        pt = jnp.where(ok, et * (1.0 / jnp.sum(et, axis=0, keepdims=True)), 0.0)
        probs.append(pt.astype(BF16))
        psum = pt if psum is None else psum + pt
    oc_t = [jnp.dot(vct, pt, preferred_element_type=F32) for pt in probs]
    gate = jnp.dot(g_ref[...], gsel_ref[0, 0], precision=HIGHEST, preferred_element_type=F32)
    oc_ref[...] = gate * jnp.concatenate(oc_t, axis=0).T

    imp = jnp.dot(c2s_ref[...], psum, precision=HIGHEST, preferred_element_type=F32)
    ns = imp.shape[0]
    jblk = lax.broadcasted_iota(jnp.int32, (ns, 1), 0)
    cur = t_lanes // SLC_BLOCK
    forced = (jblk == 0) | (jblk == cur) | (jblk == cur - 1)
    valid = jblk <= cur
    score = jnp.where(forced, SEL_BIG, jnp.where(valid, imp, -SEL_BIG))
    nb = ns // SUBLANES
    blocks = [score[b * SUBLANES:(b + 1) * SUBLANES, :] for b in range(nb)]
    jrow = lax.broadcasted_iota(jnp.int32, (SUBLANES, 1), 0)
    cnt = [jnp.zeros((SUBLANES, tq), F32) for _ in range(nb)]
    for i in range(ns):
        bi = i // SUBLANES
        si = score[i:i + 1, :]
        for b in range(nb):
            if b < bi:
                ahead = si > blocks[b]
            elif b > bi:
                ahead = si >= blocks[b]
            else:
                ahead = (si > blocks[b]) | ((si == blocks[b]) & (jrow > i - bi * SUBLANES))
            cnt[b] = cnt[b] + jnp.where(ahead, 1.0, 0.0)
    top_n = min(SLC_TOP_N, ns)
    sel_ref[0, 0] = jnp.where((jnp.concatenate(cnt, axis=0) < top_n) & valid, 1.0, 0.0).astype(sel_ref.dtype)


def nsa_cmp_sel(qkv, kvcmp, aux, gsel, c2s_t, bsz, seq, tq=256):
    n = qkv.shape[0]
    g = NSA_KV_GROUPS
    nq = seq // tq
    nc = kvcmp.shape[2]
    ns = seq // SLC_BLOCK
    row = lambda b, gi, i: b * nq + i
    vc_t = jnp.swapaxes(kvcmp[1], 1, 2)
    return pl.pallas_call(
        _cmp_sel_kernel,
        grid=(bsz, g, nq),
        in_specs=[
            pl.BlockSpec((tq, GW), lambda b, gi, i: (row(b, gi, i), gi)),
            pl.BlockSpec((1, 1, nc, GW), lambda b, gi, i: (0, b, 0, 0)),
            pl.BlockSpec((1, GW, nc), lambda b, gi, i: (b, 0, 0)),
            pl.BlockSpec((tq, GW), lambda b, gi, i: (row(b, gi, i), 2)),
            pl.BlockSpec((1, 1, GW, GW), lambda b, gi, i: (0, gi, 0, 0)),
            pl.BlockSpec((ns, nc), lambda b, gi, i: (0, 0)),
        ],
        out_specs=[
            pl.BlockSpec((tq, GW), lambda b, gi, i: (row(b, gi, i), gi)),
            pl.BlockSpec((1, 1, ns, tq), lambda b, gi, i: (b, gi, 0, i)),
        ],
        out_shape=[
            jax.ShapeDtypeStruct((n, g * GW), F32),
            jax.ShapeDtypeStruct((bsz, g, ns, seq), BF16),
        ],
        scratch_shapes=[pltpu.VMEM((GW, NSA_HEADS_PER_GROUP * tq), BF16)],
        compiler_params=_cparams("parallel", "parallel", "parallel"),
        name="nsa_cmp_sel",
    )(qkv, kvcmp, vc_t, aux, gsel, c2s_t)


M_FLOOR = -1e29


ATTN_COLS = 128


def _softmax_tile_t(s_ref, b_ref, p_ref, m, l):
    tq = b_ref.shape[1]
    m_out, l_out, a_out = [], [], []
    for c0 in range(0, s_ref.shape[1], ATTN_COLS):
        b0 = c0 % tq
        sc = s_ref[:, c0:c0 + ATTN_COLS] + b_ref[:, b0:b0 + ATTN_COLS]
        m_old = m[:, c0:c0 + ATTN_COLS]
        m_new = jnp.maximum(m_old, jnp.max(sc, axis=0, keepdims=True))
        a = jnp.exp2(m_old - m_new)
        p = jnp.exp2(sc - m_new)
        l_out.append(a * l[:, c0:c0 + ATTN_COLS] + jnp.sum(p, axis=0, keepdims=True))
        m_out.append(m_new)
        a_out.append(a)
        p_ref[:, c0:c0 + ATTN_COLS] = p.astype(BF16)
    cat = lambda parts: jnp.concatenate(parts, axis=1)
    return cat(m_out), cat(l_out), cat(a_out)


def _flash_branch_t(kt_lo, n_kt, scores, values, bufs):
    dh = NSA_HEAD_DIM
    (s_x, b_x, p_x), (s_y, b_y, p_y) = bufs
    width = s_x.shape[1]
    last = n_kt - 1

    def half_step(kt, cur, nxt, state):
        a_prev, m, l, acc = state
        scores(kt + 1, nxt[0], nxt[1])
        acc = a_prev * acc + jnp.dot(values(jnp.clip(kt - 1, kt_lo, last)), nxt[2][...],
                                     preferred_element_type=F32)
        m, l, a_cur = _softmax_tile_t(cur[0], cur[1], cur[2], m, l)
        return a_cur, m, l, acc

    def body(j, state):
        kt = kt_lo + 2 * j
        state = half_step(kt, (s_x, b_x, p_x), (s_y, b_y, p_y), state)
        return half_step(kt + 1, (s_y, b_y, p_y), (s_x, b_x, p_x), state)

    scores(kt_lo, s_x, b_x)
    p_y[...] = jnp.zeros(p_y.shape, BF16)
    init = (jnp.ones((1, width), F32), jnp.full((1, width), M_FLOOR, F32), jnp.zeros((1, width), F32),
            jnp.zeros((dh, width), F32))
    trips = (n_kt - kt_lo + 1) // 2
    a_last, _, l, acc = lax.fori_loop(0, trips, body, init)
    kt_end = kt_lo + 2 * trips - 1
    acc = a_last * acc + jnp.dot(values(jnp.minimum(kt_end, last)), p_y[...], preferred_element_type=F32)
    return acc / l


LOG2E = 1.4426950408889634
MASK_NONE, MASK_LOWER, MASK_UPPER, MASK_ALL = 0, 1, 2, 3


def _attn_kernel(tk, q_ref, ks_ref, vst_ref, kw_ref, vwt_ref, sel_ref, e_ref, cm_ref, g_ref,
                 gs_ref, gw_ref, oc_ref, o_ref, qz_ref, sx_ref, bx_ref, px_ref, sy_ref, by_ref, py_ref):
    tq = q_ref.shape[0]
    r4 = NSA_HEADS_PER_GROUP
    dh = NSA_HEAD_DIM
    g0 = pl.multiple_of(pl.program_id(1) * dh, dh)
    q0 = pl.program_id(2) * tq
    scale = dh ** -0.5 * LOG2E
    q_t = (q_ref[...].astype(F32) * scale).T.astype(BF16)
    qz_ref[...] = jnp.zeros(qz_ref.shape, BF16)
    for h in range(r4):
        qz_ref[pl.ds(g0, dh), h * tq:(h + 1) * tq] = q_t[h * dh:(h + 1) * dh, :]
    sel_bias = ((sel_ref[0, 0].astype(F32) - 1.0) * -NEG_INF).astype(BF16)
    n_kt = pl.program_id(2) + 1

    def tile_scores(k_ref, kt, s_out):
        k0 = pl.multiple_of(jnp.minimum(kt, n_kt - 1) * tk, tk)
        s_out[...] = jnp.dot(k_ref[pl.ds(k0, tk), :], qz_ref[...], preferred_element_type=F32)

    def sel_scores(kt, s_out, b_out):
        tile_scores(ks_ref, kt, s_out)
        which = jnp.where(kt >= n_kt, MASK_ALL, jnp.where(kt == n_kt - 1, MASK_LOWER, MASK_NONE))
        b_out[...] = jnp.dot(e_ref[jnp.minimum(kt, n_kt - 1)], sel_bias,
                             preferred_element_type=F32) + cm_ref[which]

    def win_scores(kt, s_out, b_out):
        tile_scores(kw_ref, kt, s_out)
        back = n_kt - 1 - kt
        which = jnp.where(kt >= n_kt, MASK_ALL,
                          jnp.where(back == 0, MASK_LOWER, jnp.where(back == 1, MASK_NONE, MASK_UPPER)))
        b_out[...] = cm_ref[which]

    bufs = ((sx_ref, bx_ref, px_ref), (sy_ref, by_ref, py_ref))
    o_s = _flash_branch_t(0, n_kt, sel_scores, lambda kt: vst_ref[0, kt, pl.ds(g0, dh), :], bufs)
    kt_lo = jnp.maximum(n_kt - 1 - WINDOW // tk, 0)
    o_w = _flash_branch_t(kt_lo, n_kt, win_scores, lambda kt: vwt_ref[0, kt, pl.ds(g0, dh), :], bufs)

    def branch_out(o_t):
        return jnp.concatenate([o_t[:, h * tq:(h + 1) * tq] for h in range(r4)], axis=0).T

    state_s, state_w = o_s, o_w
    gates = g_ref[...]
    g_s = jnp.dot(gates, gs_ref[0, 0], precision=HIGHEST, preferred_element_type=F32)
    g_w = jnp.dot(gates, gw_ref[0, 0], precision=HIGHEST, preferred_element_type=F32)
    o_ref[...] = (oc_ref[...] + g_s * branch_out(state_s) + g_w * branch_out(state_w)).astype(o_ref.dtype)


def nsa_attn(qkv, aux, sel_t, oc, gsel, e3t, bsz, seq, tq=256, tk=256):
    n = qkv.shape[0]
    g = NSA_KV_GROUPS
    r4 = NSA_HEADS_PER_GROUP
    nq = seq // tq
    nkt = seq // tk
    ns = seq // SLC_BLOCK
    nqt = NSA_HEADS * NSA_HEAD_DIM // GW
    assert tq == tk and WINDOW == 2 * tk, "the constant mask table assumes square tiles and a 2-tile window"
    row = lambda b, gi, i: b * nq + i
    kv = lambda c: pl.BlockSpec((seq, GW), lambda b, gi, i: (b, nqt + c))
    lower = jnp.arange(tk)[:, None] <= jnp.arange(tq)[None, :]
    keep = jnp.stack([jnp.ones_like(lower), lower, ~lower, jnp.zeros_like(lower)])
    cmask = jnp.where(keep, 0.0, NEG_INF).astype(F32)
    vt = lambda c: qkv[:, (nqt + c) * GW:(nqt + c + 1) * GW].reshape(bsz, nkt, tk, GW).transpose(0, 1, 3, 2)
    vt_spec = pl.BlockSpec((1, nkt, GW, tk), lambda b, gi, i: (b, 0, 0, 0))
    return pl.pallas_call(
        functools.partial(_attn_kernel, tk),
        grid=(bsz, g, nq),
        in_specs=[
            pl.BlockSpec((tq, GW), lambda b, gi, i: (row(b, gi, i), gi)),
            kv(0), vt_spec, kv(2), vt_spec,
            pl.BlockSpec((1, 1, ns, tq), lambda b, gi, i: (b, gi, 0, i)),
            pl.BlockSpec((nkt, tk, ns), lambda b, gi, i: (0, 0, 0)),
            pl.BlockSpec((4, tk, tq), lambda b, gi, i: (0, 0, 0)),
            pl.BlockSpec((tq, GW), lambda b, gi, i: (row(b, gi, i), 2)),
            pl.BlockSpec((1, 1, GW, GW), lambda b, gi, i: (1, gi, 0, 0)),
            pl.BlockSpec((1, 1, GW, GW), lambda b, gi, i: (2, gi, 0, 0)),
            pl.BlockSpec((tq, GW), lambda b, gi, i: (row(b, gi, i), gi)),
        ],
        out_specs=pl.BlockSpec((tq, GW), lambda b, gi, i: (row(b, gi, i), gi)),
        out_shape=jax.ShapeDtypeStruct((n, g * GW), BF16),
        scratch_shapes=[pltpu.VMEM((GW, r4 * tq), BF16)] + 2 * [
            pltpu.VMEM((tk, r4 * tq), F32),
            pltpu.VMEM((tk, tq), F32),
            pltpu.VMEM((tk, r4 * tq), BF16),
        ],
        compiler_params=_cparams("parallel", "parallel", "parallel"),
        name="nsa_attn",
    )(qkv, qkv, vt(1), qkv, vt(3), sel_t, e3t, cmask, aux, gsel, gsel, oc)


def _out_proj_kernel(alpha, a_ref, x_ref, gm_ref, w_ref, lg_ref, lb_ref, o_ref):
    y = jnp.dot(a_ref[...], w_ref[...], preferred_element_type=F32)
    o_ref[...] = _layer_norm(alpha * x_ref[...] + (1.0 + gm_ref[0]) * y, lg_ref[...], lb_ref[...])


def out_proj_ln(a_bf, x2, mod3, chunk, w_bf, ln_g, ln_b, seq, alpha, tm=512):
    n, d = x2.shape
    k = a_bf.shape[1]
    tpb = seq // tm
    return pl.pallas_call(
        functools.partial(_out_proj_kernel, alpha),
        grid=(n // tm,),
        in_specs=[
            pl.BlockSpec((tm, k), lambda i: (i, 0)),
            pl.BlockSpec((tm, d), lambda i: (i, 0)),
            _mod_spec(d, tpb, chunk, 1),
            pl.BlockSpec((k, d), lambda i: (0, 0)),
            pl.BlockSpec((1, d), lambda i: (0, 0)),
            pl.BlockSpec((1, d), lambda i: (0, 0)),
        ],
        out_specs=pl.BlockSpec((tm, d), lambda i: (i, 0)),
        out_shape=jax.ShapeDtypeStruct((n, d), F32),
        compiler_params=_cparams("parallel"),
        name="out_proj_ln",
    )(a_bf, x2, mod3, w_bf, _row(ln_g), _row(ln_b))


def _nsa_constants(seq, tk):
    g, r4, dh = NSA_KV_GROUPS, NSA_HEADS_PER_GROUP, NSA_HEAD_DIM
    lane = jnp.arange(GW)
    col = (jnp.arange(g)[:, None] * r4 + lane[None, :] // dh) * 3
    gsel = (lane[None, None, :, None] == (col[None, :, None, :] + jnp.arange(3)[:, None, None, None]))
    nc = seq // CMP_STRIDE
    ns = seq // SLC_BLOCK
    sub = SLC_BLOCK // CMP_STRIDE
    seg = jnp.arange(nc)
    c2s_t = sum(((seg[None, :] + o) // sub == jnp.arange(ns)[:, None]).astype(F32)
                for o in range(CMP_BLOCK // CMP_STRIDE))
    key = jnp.arange(seq).reshape(seq // tk, tk, 1)
    e3 = (key // SLC_BLOCK == jnp.arange(ns)[None, None, :])
    return gsel.astype(F32), c2s_t, e3.astype(BF16)


def nsa_layer(x2, mod3, positions, w_in, cmp_pos, cmp_w1, cmp_w2, w_out, ln_g, ln_b, seq, alpha,
              tq=256, tk=256):
    n, d = x2.shape
    bsz = n // seq
    g, dh = NSA_KV_GROUPS, NSA_HEAD_DIM
    qw, kvw = NSA_HEADS * dh, g * dh
    cols = lambda k: w_in[:, qw + k * kvw: qw + (k + 1) * kvw]
    gl = w_in[:, qw + 6 * kvw:]
    w_main = jnp.concatenate([w_in[:, :qw], cols(2), cols(3), cols(4), cols(5)], axis=1).astype(BF16)
    w_aux = jnp.concatenate([cols(0), cols(1), jnp.pad(gl, ((0, 0), (0, GW - gl.shape[1])))],
                            axis=1).astype(BF16)
    tables = _rope_tables(positions)
    nqt = qw // GW
    kinds_main = (ROPE,) * nqt + (ROPE, PLAIN, ROPE, PLAIN)
    qkv = nsa_proj(x2, mod3, w_main, tables, kinds_main, BF16, seq, "nsa_proj_main")
    aux = nsa_proj(x2, mod3, w_aux, tables, (PLAIN, PLAIN, SIGMOID), F32, seq, "nsa_proj_aux")

    hid = cmp_w1.shape[-1]
    eye = jnp.eye(g, dtype=F32)
    w1bd = jnp.einsum('spdh,gk->spgdkh', cmp_w1.reshape(2, CMP_BLOCK, dh, hid), eye)
    w1bd = w1bd.reshape(2, CMP_BLOCK, g * dh, g * hid).astype(BF16)
    w2bd = jnp.einsum('shd,gk->sghkd', cmp_w2, eye).reshape(2, g * hid, g * dh).astype(BF16)
    pos_t = jnp.tile(cmp_pos, (1, 1, g))
    nh = seq // CMP_STRIDE
    cmp_positions = jnp.pad(positions[:, CMP_BLOCK - 1::CMP_STRIDE], ((0, 0), (0, 1)))[:, :nh]
    ctabs = tuple(t.reshape(bsz, nh, LANES) for t in _rope_tables(cmp_positions))
    kvcmp = nsa_compress(aux, w1bd, w2bd, pos_t[:, :CMP_STRIDE], pos_t[:, CMP_STRIDE:], ctabs, bsz, seq)

    gsel, c2s_t, e3 = _nsa_constants(seq, tk)
    oc, sel_t = nsa_cmp_sel(qkv, kvcmp, aux, gsel, c2s_t, bsz, seq, tq)
    o = nsa_attn(qkv, aux, sel_t, oc, gsel, e3, bsz, seq, tq, tk)
    return out_proj_ln(o, x2, mod3, 2, w_out.astype(BF16), ln_g, ln_b, seq, alpha)


MOE_ROWS = 512
RINFO = 8


def _router_kernel(x_ref, sh_ref, sc_ref, r_ref, tril_ref, info_ref, cnt_ref, carry_ref):
    tm = x_ref.shape[0]
    i = pl.program_id(0)

    @pl.when(i == 0)
    def _():
        carry_ref[...] = jnp.zeros_like(carry_ref)

    u = x_ref[...] * (1.0 + sc_ref[0]) + sh_ref[0]
    logits = jnp.dot(u, r_ref[...], precision=HIGHEST, preferred_element_type=F32)
    lane = lax.broadcasted_iota(jnp.int32, (1, LANES), 1)
    logits = jnp.where(lane < N_EXPERTS, logits, -jnp.inf)
    m1 = jnp.max(logits, axis=-1, keepdims=True)
    e1 = jnp.min(jnp.where(logits == m1, lane, LANES), axis=-1, keepdims=True)
    rest = jnp.where(lane == e1, -jnp.inf, logits)
    m2 = jnp.max(rest, axis=-1, keepdims=True)
    e2 = jnp.min(jnp.where(rest == m2, lane, LANES), axis=-1, keepdims=True)
    ex = jnp.exp(m2 - m1)
    g1 = 1.0 / (1.0 + ex)
    g2 = ex / (1.0 + ex)
    o1 = lane == e1
    o2 = lane == e2
    tot = jnp.where(o1 | o2, 1.0, 0.0)
    before = carry_ref[...] + jnp.dot(tril_ref[...], tot.astype(BF16), preferred_element_type=F32)
    rank1 = jnp.sum(jnp.where(o1, before, 0.0), axis=-1, keepdims=True)
    rank2 = jnp.sum(jnp.where(o2, before, 0.0), axis=-1, keepdims=True)
    carry_ref[...] += jnp.sum(tot, axis=0, keepdims=True)
    cnt_ref[...] = carry_ref[...]
    li = lax.broadcasted_iota(jnp.int32, (1, RINFO), 1)
    info = jnp.zeros((tm, RINFO), F32)
    for k, v in enumerate((e1.astype(F32), e2.astype(F32), g1, g2, rank1, rank2)):
        info = jnp.where(li == k, v, info)
    info_ref[...] = info


def moe_router(x2, mod3, router, seq, tm=512):
    n, d = x2.shape
    tpb = seq // tm
    r_pad = jnp.pad(router, ((0, 0), (0, LANES - router.shape[1])))
    tril = (jnp.arange(tm)[:, None] > jnp.arange(tm)[None, :]).astype(BF16)
    return pl.pallas_call(
        _router_kernel,
        grid=(n // tm,),
        in_specs=[
            pl.BlockSpec((tm, d), lambda i: (i, 0)),
            _mod_spec(d, tpb, 3, 1), _mod_spec(d, tpb, 4, 1),
            pl.BlockSpec((d, LANES), lambda i: (0, 0)),
            pl.BlockSpec((tm, tm), lambda i: (0, 0)),
        ],
        out_specs=[
            pl.BlockSpec((tm, RINFO), lambda i: (i, 0)),
            pl.BlockSpec((1, LANES), lambda i: (0, 0)),
        ],
        out_shape=[
            jax.ShapeDtypeStruct((n, RINFO), F32),
            jax.ShapeDtypeStruct((1, LANES), F32),
        ],
        scratch_shapes=[pltpu.VMEM((1, LANES), F32)],
        compiler_params=_cparams("arbitrary"),
        name="moe_router",
    )(x2, mod3, mod3, r_pad, tril)


def _dispatch_kernel(dest_ref, x_ref, sh_ref, sc_ref, xs_in_ref, xs_ref, u_ref, sem):
    del xs_in_ref
    tm = x_ref.shape[0]
    u_ref[...] = x_ref[...] * (1.0 + sc_ref[0]) + sh_ref[0]

    def row_copy(r, k):
        return pltpu.make_async_copy(u_ref.at[pl.ds(r, 1)], xs_ref.at[pl.ds(dest_ref[TOP_K * r + k], 1)], sem)

    def issue(r, carry):
        for k in range(TOP_K):
            row_copy(r, k).start()
        return carry

    def drain(r, carry):
        for k in range(TOP_K):
            row_copy(r, k).wait()
        return carry

    lax.fori_loop(0, tm, issue, 0)
    lax.fori_loop(0, tm, drain, 0)


def moe_dispatch(x2, mod3, dest, cap, seq, tm=256):
    n, d = x2.shape
    tpb = seq // tm
    xs0 = jnp.zeros((cap, d), F32)
    return pl.pallas_call(
        _dispatch_kernel,
        grid=(n // tm,),
        in_specs=[
            pl.BlockSpec((TOP_K * tm,), lambda i: (i,), memory_space=pltpu.SMEM),
            pl.BlockSpec((tm, d), lambda i: (i, 0)),
            _mod_spec(d, tpb, 3, 1), _mod_spec(d, tpb, 4, 1),
            pl.BlockSpec(memory_space=pl.ANY),
        ],
        out_specs=pl.BlockSpec(memory_space=pl.ANY),
        out_shape=jax.ShapeDtypeStruct((cap, d), F32),
        scratch_shapes=[pltpu.VMEM((tm, d), F32), pltpu.SemaphoreType.DMA],
        input_output_aliases={4: 0},
        compiler_params=_cparams("arbitrary"),
        name="moe_dispatch",
    )(dest, x2, mod3, mod3, xs0)


def _moe_ffn_kernel(bexp_ref, nused_ref, xs_ref, w1_ref, w3_ref, w2_ref, o_ref, xb_ref, acc_ref):
    del bexp_ref
    f = pl.program_id(1)
    used = pl.program_id(0) < nused_ref[0]

    @pl.when(used & (f == 0))
    def _():
        xb_ref[...] = xs_ref[...].astype(BF16)

    @pl.when(used)
    def _():
        xb = xb_ref[...]
        h = _silu(jnp.dot(xb, w1_ref[0], preferred_element_type=F32)) * jnp.dot(
            xb, w3_ref[0], preferred_element_type=F32)
        part = jnp.dot(h.astype(BF16), w2_ref[0], preferred_element_type=F32)

        @pl.when(f == 0)
        def _():
            acc_ref[...] = part

        @pl.when(f > 0)
        def _():
            acc_ref[...] += part

    @pl.when(f == pl.num_programs(1) - 1)
    def _():
        o_ref[...] = jnp.where(used, acc_ref[...], 0.0)


def moe_ffn(xs, blk_exp, n_used, w1_bf, w3_bf, w2_bf, tf=512):
    cap, d = xs.shape
    ff = w1_bf.shape[2]
    grid_spec = pltpu.PrefetchScalarGridSpec(
        num_scalar_prefetch=2,
        grid=(cap // MOE_ROWS, ff // tf),
        in_specs=[
            pl.BlockSpec((MOE_ROWS, d), lambda i, f, be, nu: (i, 0)),
            pl.BlockSpec((1, d, tf), lambda i, f, be, nu: (be[i], 0, f)),
            pl.BlockSpec((1, d, tf), lambda i, f, be, nu: (be[i], 0, f)),
            pl.BlockSpec((1, tf, d), lambda i, f, be, nu: (be[i], f, 0)),
        ],
        out_specs=pl.BlockSpec((MOE_ROWS, d), lambda i, f, be, nu: (i, 0)),
        scratch_shapes=[pltpu.VMEM((MOE_ROWS, d), BF16), pltpu.VMEM((MOE_ROWS, d), F32)],
    )
    return pl.pallas_call(
        _moe_ffn_kernel,
        grid_spec=grid_spec,
        out_shape=jax.ShapeDtypeStruct((cap, d), F32),
        compiler_params=_cparams("parallel", "arbitrary"),
        name="moe_ffn",
    )(blk_exp, n_used, xs, w1_bf, w3_bf, w2_bf)


def _combine_kernel(alpha, dest_ref, info_ref, x_ref, gf_ref, ys_ref, lg_ref, lb_ref, o_ref, buf_ref, sem):
    tm = x_ref.shape[0]

    def row_copy(r, k):
        return pltpu.make_async_copy(ys_ref.at[pl.ds(dest_ref[TOP_K * r + k], 1)],
                                     buf_ref.at[k, pl.ds(r, 1)], sem)

    def issue(r, carry):
        for k in range(TOP_K):
            row_copy(r, k).start()
        return carry

    def drain(r, carry):
        for k in range(TOP_K):
            row_copy(r, k).wait()
        return carry

    lax.fori_loop(0, tm, issue, 0)
    lax.fori_loop(0, tm, drain, 0)
    info = info_ref[...]
    y = info[:, 2:3] * buf_ref[0] + info[:, 3:4] * buf_ref[1]
    o_ref[...] = _layer_norm(alpha * x_ref[...] + (1.0 + gf_ref[0]) * y, lg_ref[...], lb_ref[...])


def moe_combine(ys, dest, info, x2, mod3, ln_g, ln_b, seq, alpha, tm=256):
    n, d = x2.shape
    tpb = seq // tm
    return pl.pallas_call(
        functools.partial(_combine_kernel, alpha),
        grid=(n // tm,),
        in_specs=[
            pl.BlockSpec((TOP_K * tm,), lambda i: (i,), memory_space=pltpu.SMEM),
            pl.BlockSpec((tm, RINFO), lambda i: (i, 0)),
            pl.BlockSpec((tm, d), lambda i: (i, 0)),
            _mod_spec(d, tpb, 5, 1),
            pl.BlockSpec(memory_space=pl.ANY),
            pl.BlockSpec((1, d), lambda i: (0, 0)),
            pl.BlockSpec((1, d), lambda i: (0, 0)),
        ],
        out_specs=pl.BlockSpec((tm, d), lambda i: (i, 0)),
        out_shape=jax.ShapeDtypeStruct((n, d), F32),
        scratch_shapes=[pltpu.VMEM((TOP_K, tm, d), F32), pltpu.SemaphoreType.DMA],
        compiler_params=_cparams("arbitrary"),
        name="moe_combine",
    )(dest, info, x2, mod3, ys, _row(ln_g), _row(ln_b))


def moe_layer(x2, mod3, router, w1, w3, w2, ln_g, ln_b, seq, alpha):
    n, d = x2.shape
    info, counts = moe_router(x2, mod3, router, seq)
    sizes = counts[0, :N_EXPERTS].astype(jnp.int32)
    padded = (sizes + MOE_ROWS - 1) // MOE_ROWS * MOE_ROWS
    pend = jnp.cumsum(padded)
    pstart = pend - padded
    cap = n * TOP_K + N_EXPERTS * MOE_ROWS
    n_blk = cap // MOE_ROWS
    blk_lo = jnp.arange(n_blk, dtype=jnp.int32) * MOE_ROWS
    blk_exp = jnp.minimum(jnp.sum(blk_lo[:, None] >= pend[None, :], axis=1), N_EXPERTS - 1).astype(jnp.int32)
    n_used = (pend[-1:] // MOE_ROWS).astype(jnp.int32)
    ids = jnp.arange(N_EXPERTS, dtype=jnp.int32)
    e = info[:, 0:TOP_K].astype(jnp.int32)
    start = jnp.sum(jnp.where(e[:, :, None] == ids, pstart, 0), axis=-1)
    dest = (start + info[:, 4:4 + TOP_K].astype(jnp.int32)).reshape(-1)
    xs = moe_dispatch(x2, mod3, dest, cap, seq)
    ys = moe_ffn(xs, blk_exp, n_used, w1.astype(BF16), w3.astype(BF16), w2.astype(BF16))
    return moe_combine(ys, dest, info, x2, mod3, ln_g, ln_b, seq, alpha)


def kernel(x, c, positions, ada_w, ada_b, ln_g, ln_b, conv_w_in, conv_b_in, conv_dw, conv_dw_b, conv_ln_g, conv_ln_b, conv_w_out, conv_b_out, nsa_w_in, nsa_cmp_pos, nsa_cmp_w1, nsa_cmp_w2, nsa_w_out, pool_w, pool_scale, ffn_w1, ffn_w3, ffn_w2, moe_router, moe_w1, moe_w3, moe_w2):
    bsz, seq, d = x.shape
    depth = ada_w.shape[0]
    alpha = (2 * depth) ** 0.25
    mod = ada_mod(c, ada_w, ada_b)
    x2 = x.reshape(bsz * seq, d)
    mods = mod.reshape(depth, bsz, 1, 6 * d)
    for i in range(depth):
        mod3 = mods[i]
        j = i // N_MIXERS
        if i % N_MIXERS == 0:
            h = conv_in(x2, mod3, conv_w_in[j].astype(BF16), conv_b_in[j], seq)
            x2 = conv_tail(h, x2, mod3, conv_dw[j], conv_dw_b[j], conv_ln_g[j], conv_ln_b[j],
                           conv_w_out[j].astype(BF16), conv_b_out[j], ln_g[i, 0], ln_b[i, 0], seq, alpha)
        elif i % N_MIXERS == 1:
            x2 = nsa_layer(x2, mod3, positions, nsa_w_in[j], nsa_cmp_pos[j], nsa_cmp_w1[j], nsa_cmp_w2[j],
                           nsa_w_out[j], ln_g[i, 0], ln_b[i, 0], seq, alpha)
        else:
            x2 = pool_layer(x2, mod3, pool_w[j].astype(BF16), pool_scale[j], ln_g[i, 0], ln_b[i, 0],
                            seq, alpha)
        j = i // 2
        if i % 2 == 0:
            x2 = ffn_dense(x2, mod3, ffn_w1[j].astype(BF16), ffn_w3[j].astype(BF16),
                           ffn_w2[j].astype(BF16), ln_g[i, 1], ln_b[i, 1], seq, alpha)
        else:
            x2 = moe_layer(x2, mod3, moe_router[j], moe_w1[j], moe_w3[j], moe_w2[j],
                           ln_g[i, 1], ln_b[i, 1], seq, alpha)
    return x2.reshape(bsz, seq, d)
```

```python
import functools

import jax
import jax.numpy as jnp
from jax import lax
from jax.experimental import pallas as pl
from jax.experimental.pallas import tpu as pltpu

F32 = jnp.float32
BF16 = jnp.bfloat16
HIGHEST = lax.Precision.HIGHEST

N_MIXERS = 3
CONV_WIDTH = 31
NSA_HEADS = 16
NSA_KV_GROUPS = 4
NSA_HEADS_PER_GROUP = NSA_HEADS // NSA_KV_GROUPS
NSA_HEAD_DIM = 64
CMP_BLOCK = 32
CMP_STRIDE = 16
SLC_BLOCK = 64
SLC_TOP_N = 16
WINDOW = 512
ROPE_THETA = 500000.0
ROT_DIM = NSA_HEAD_DIM // 4
POOL_WINDOWS = (2, 4, 8, 16)
N_EXPERTS = 8
TOP_K = 2
LN_EPS = 1e-5
NEG_INF = -1e30
SEL_BIG = 1e9

LANES = 128
SUBLANES = 8
VMEM_LIMIT = 56 * 1024 * 1024


def _cparams(*sem):
    return pltpu.CompilerParams(dimension_semantics=sem, vmem_limit_bytes=VMEM_LIMIT)


def _layer_norm(v, g, b):
    mu = jnp.mean(v, axis=-1, keepdims=True)
    d = v - mu
    var = jnp.mean(d * d, axis=-1, keepdims=True)
    return d * lax.rsqrt(var + LN_EPS) * g + b


def _silu(v):
    return v * jax.nn.sigmoid(v)


def _row(v):
    return v.reshape(1, -1)


def _ada_kernel(c_ref, w_ref, b_ref, o_ref):
    cond = _silu(c_ref[...])
    o_ref[0] = jnp.dot(cond, w_ref[0], precision=HIGHEST, preferred_element_type=F32) + b_ref[0]


def ada_mod(c, ada_w, ada_b):
    depth, d, d6 = ada_w.shape
    b = c.shape[0]
    tn = d6 // 4
    return pl.pallas_call(
        _ada_kernel,
        grid=(depth, d6 // tn),
        in_specs=[
            pl.BlockSpec((b, d), lambda i, j: (0, 0)),
            pl.BlockSpec((1, d, tn), lambda i, j: (i, 0, j)),
            pl.BlockSpec((1, 1, tn), lambda i, j: (i, 0, j)),
        ],
        out_specs=pl.BlockSpec((1, b, tn), lambda i, j: (i, 0, j)),
        out_shape=jax.ShapeDtypeStruct((depth, b, d6), F32),
        compiler_params=_cparams("parallel", "parallel"),
        name="ada_mod",
    )(c, ada_w, ada_b.reshape(depth, 1, d6))


def _mod_spec(d, tiles_per_batch, chunk, ngrid):
    if ngrid == 1:
        return pl.BlockSpec((1, 1, d), lambda i: (i // tiles_per_batch, 0, chunk))
    return pl.BlockSpec((1, 1, d), lambda i, j: (i // tiles_per_batch, 0, chunk))


CONV_IN_COLS = 512


def _conv_in_kernel(x_ref, sh_ref, sc_ref, w_ref, b_ref, h_ref):
    d = h_ref.shape[1]
    u = (x_ref[...] * (1.0 + sc_ref[0]) + sh_ref[0]).astype(BF16)
    for c0 in range(0, d, CONV_IN_COLS):
        a = jnp.dot(u, w_ref[:, c0:c0 + CONV_IN_COLS], preferred_element_type=F32) + b_ref[:, c0:c0 + CONV_IN_COLS]
        g = jnp.dot(u, w_ref[:, d + c0:d + c0 + CONV_IN_COLS], preferred_element_type=F32) + b_ref[
            :, d + c0:d + c0 + CONV_IN_COLS]
        h_ref[:, c0:c0 + CONV_IN_COLS] = a * jax.nn.sigmoid(g)


def conv_in(x2, mod3, w_in_bf, b_in, seq, tm=512):
    n, d = x2.shape
    tpb = seq // tm
    return pl.pallas_call(
        _conv_in_kernel,
        grid=(n // tm,),
        in_specs=[
            pl.BlockSpec((tm, d), lambda i: (i, 0)),
            _mod_spec(d, tpb, 0, 1),
            _mod_spec(d, tpb, 1, 1),
            pl.BlockSpec((d, 2 * d), lambda i: (0, 0)),
            pl.BlockSpec((1, 2 * d), lambda i: (0, 0)),
        ],
        out_specs=pl.BlockSpec((tm, d), lambda i: (i, 0)),
        out_shape=jax.ShapeDtypeStruct((n, d), F32),
        compiler_params=_cparams("parallel"),
        name="conv_in",
    )(x2, mod3, mod3, w_in_bf, _row(b_in))


CONV_HALO = 32
CONV_ROWS = 16


def _conv_tail_kernel(tpb, alpha, hc_ref, hp_ref, x_ref, gm_ref, dw_ref, dwb_ref, cg_ref, cb_ref,
                      wo_ref, bo_ref, lg_ref, lb_ref, o_ref, win_ref, cv_ref):
    tm = hc_ref.shape[0]
    first = (pl.program_id(0) % tpb) == 0
    win_ref[0, 0:CONV_HALO, :] = jnp.where(first, 0.0, hp_ref[...])
    win_ref[0, CONV_HALO:CONV_HALO + tm, :] = hc_ref[...]
    keep = tm + CONV_HALO - SUBLANES
    for s in range(1, SUBLANES):
        win_ref[s, 0:keep, :] = win_ref[0, s:s + keep, :]
    off = CONV_HALO - (CONV_WIDTH - 1)
    halves = CONV_ROWS // SUBLANES
    for rc in range(tm // CONV_ROWS):
        r0 = rc * CONV_ROWS
        accs = [None] * halves
        for k in range(CONV_WIDTH):
            j = off + k
            a0 = r0 + j - j % SUBLANES
            tap = dw_ref[k]
            for i in range(halves):
                term = win_ref[j % SUBLANES, a0 + i * SUBLANES:a0 + (i + 1) * SUBLANES, :] * tap
                accs[i] = term + dwb_ref[...] if accs[i] is None else accs[i] + term
        for i in range(halves):
            cv_ref[r0 + i * SUBLANES:r0 + (i + 1) * SUBLANES, :] = accs[i]
    hn = _silu(_layer_norm(cv_ref[...], cg_ref[...], cb_ref[...]))
    y = jnp.dot(hn.astype(BF16), wo_ref[...], preferred_element_type=F32) + bo_ref[...]
    o_ref[...] = _layer_norm(alpha * x_ref[...] + (1.0 + gm_ref[0]) * y, lg_ref[...], lb_ref[...])


def conv_tail(h, x2, mod3, dw, dw_b, cln_g, cln_b, w_out_bf, b_out, ln_g, ln_b, seq, alpha, tm=256):
    n, d = x2.shape
    tpb = seq // tm
    hb = tm // CONV_HALO
    full = lambda shape: pl.BlockSpec(shape, lambda i: (0, 0))
    return pl.pallas_call(
        functools.partial(_conv_tail_kernel, tpb, alpha),
        grid=(n // tm,),
        in_specs=[
            pl.BlockSpec((tm, d), lambda i: (i, 0)),
            pl.BlockSpec((CONV_HALO, d), lambda i: (jnp.maximum(i * hb - 1, 0), 0)),
            pl.BlockSpec((tm, d), lambda i: (i, 0)),
            _mod_spec(d, tpb, 2, 1),
            pl.BlockSpec((CONV_WIDTH, SUBLANES, d), lambda i: (0, 0, 0)), full((1, d)), full((1, d)), full((1, d)),
            full((d, d)), full((1, d)), full((1, d)), full((1, d)),
        ],
        out_specs=pl.BlockSpec((tm, d), lambda i: (i, 0)),
        out_shape=jax.ShapeDtypeStruct((n, d), F32),
        scratch_shapes=[pltpu.VMEM((SUBLANES, tm + CONV_HALO, d), F32), pltpu.VMEM((tm, d), F32)],
        compiler_params=_cparams("parallel"),
        name="conv_tail",
    )(h, h, x2, mod3, jnp.broadcast_to(dw[:, None, :], (CONV_WIDTH, SUBLANES, d)), _row(dw_b), _row(cln_g),
      _row(cln_b), w_out_bf, _row(b_out), _row(ln_g), _row(ln_b))


FFN_ROWS = 256


def _swiglu_accumulate(u_ref, w1, w3, w2, acc_ref):
    for r0 in range(0, u_ref.shape[0], FFN_ROWS):
        u = u_ref[r0:r0 + FFN_ROWS, :]
        h = _silu(jnp.dot(u, w1, preferred_element_type=F32)) * jnp.dot(u, w3, preferred_element_type=F32)
        acc_ref[r0:r0 + FFN_ROWS, :] += jnp.dot(h.astype(BF16), w2, preferred_element_type=F32)


def _ffn_kernel(alpha, x_ref, sh_ref, sc_ref, gf_ref, w1_ref, w3_ref, w2_ref, lg_ref, lb_ref,
                o_ref, u_ref, acc_ref):
    f = pl.program_id(1)

    @pl.when(f == 0)
    def _():
        u_ref[...] = (x_ref[...] * (1.0 + sc_ref[0]) + sh_ref[0]).astype(BF16)
        acc_ref[...] = jnp.zeros(acc_ref.shape, F32)

    _swiglu_accumulate(u_ref, w1_ref[0].astype(BF16), w3_ref[0].astype(BF16), w2_ref[0].astype(BF16), acc_ref)

    @pl.when(f == pl.num_programs(1) - 1)
    def _():
        o_ref[...] = _layer_norm(alpha * x_ref[...] + (1.0 + gf_ref[0]) * acc_ref[...],
                                 lg_ref[...], lb_ref[...])


def ffn_dense(x2, mod3, w1, w3, w2, layer, ln_g, ln_b, seq, alpha, tm=1024, tf=512):
    n, d = x2.shape
    ff = w1.shape[2]
    tpb = seq // tm
    return pl.pallas_call(
        functools.partial(_ffn_kernel, alpha),
        grid=(n // tm, ff // tf),
        in_specs=[
            pl.BlockSpec((tm, d), lambda i, f: (i, 0)),
            _mod_spec(d, tpb, 3, 2), _mod_spec(d, tpb, 4, 2), _mod_spec(d, tpb, 5, 2),
            pl.BlockSpec((1, d, tf), lambda i, f: (layer, 0, f)),
            pl.BlockSpec((1, d, tf), lambda i, f: (layer, 0, f)),
            pl.BlockSpec((1, tf, d), lambda i, f: (layer, f, 0)),
            pl.BlockSpec((1, d), lambda i, f: (0, 0)),
            pl.BlockSpec((1, d), lambda i, f: (0, 0)),
        ],
        out_specs=pl.BlockSpec((tm, d), lambda i, f: (i, 0)),
        out_shape=jax.ShapeDtypeStruct((n, d), F32),
        scratch_shapes=[pltpu.VMEM((tm, d), BF16), pltpu.VMEM((tm, d), F32)],
        compiler_params=_cparams("parallel", "arbitrary"),
        name="ffn_dense",
    )(x2, mod3, mod3, mod3, w1, w3, w2, _row(ln_g), _row(ln_b))


POOL_HALO = 16


def _pool_kernel(tpb, alpha, xc_ref, xp_ref, sh_ref, sc_ref, gm_ref, w_ref, ps_ref, lg_ref, lb_ref,
                 o_ref, win_ref):
    tm, d = xc_ref.shape
    gw = d // len(POOL_WINDOWS)
    tile = pl.program_id(0) % tpb
    first = tile == 0
    scale = 1.0 + sc_ref[0]
    up = xp_ref[...] * scale + sh_ref[0]
    uc = xc_ref[...] * scale + sh_ref[0]
    win_ref[0:POOL_HALO, :] = jnp.where(first, 0.0, up)
    win_ref[POOL_HALO:POOL_HALO + tm, :] = uc
    t = tile * tm + lax.broadcasted_iota(jnp.int32, (tm, 1), 0)
    ys = []
    for gi, w in enumerate(POOL_WINDOWS):
        c0 = gi * gw
        s = uc[:, c0:c0 + gw]
        for k in range(1, w):
            s = s + win_ref[POOL_HALO - k:POOL_HALO - k + tm, c0:c0 + gw]
        cnt = jnp.minimum(t + 1, w).astype(F32)
        p = s / cnt - uc[:, c0:c0 + gw]
        ys.append(jnp.dot(p.astype(BF16), w_ref[gi], preferred_element_type=F32))
    y = jnp.concatenate(ys, axis=-1) * ps_ref[...]
    o_ref[...] = _layer_norm(alpha * xc_ref[...] + (1.0 + gm_ref[0]) * y, lg_ref[...], lb_ref[...])


def pool_layer(x2, mod3, w_grp_bf, pool_scale, ln_g, ln_b, seq, alpha, tm=512):
    n, d = x2.shape
    tpb = seq // tm
    hb = tm // POOL_HALO
    ng, gw, _ = w_grp_bf.shape
    return pl.pallas_call(
        functools.partial(_pool_kernel, tpb, alpha),
        grid=(n // tm,),
        in_specs=[
            pl.BlockSpec((tm, d), lambda i: (i, 0)),
            pl.BlockSpec((POOL_HALO, d), lambda i: (jnp.maximum(i * hb - 1, 0), 0)),
            _mod_spec(d, tpb, 0, 1), _mod_spec(d, tpb, 1, 1), _mod_spec(d, tpb, 2, 1),
            pl.BlockSpec((ng, gw, gw), lambda i: (0, 0, 0)),
            pl.BlockSpec((1, d), lambda i: (0, 0)),
            pl.BlockSpec((1, d), lambda i: (0, 0)),
            pl.BlockSpec((1, d), lambda i: (0, 0)),
        ],
        out_specs=pl.BlockSpec((tm, d), lambda i: (i, 0)),
        out_shape=jax.ShapeDtypeStruct((n, d), F32),
        scratch_shapes=[pltpu.VMEM((tm + POOL_HALO, d), F32)],
        compiler_params=_cparams("parallel"),
        name="pool_layer",
    )(x2, x2, mod3, mod3, mod3, w_grp_bf, _row(pool_scale), _row(ln_g), _row(ln_b))


GW = NSA_HEADS_PER_GROUP * NSA_HEAD_DIM
PLAIN, ROPE, SIGMOID = 0, 1, 2


def _rope_tables(positions):
    half = ROT_DIM // 2
    inv = ROPE_THETA ** (-jnp.arange(half, dtype=F32) * 2.0 / ROT_DIM)
    ang = positions.astype(F32)[..., None] * inv
    cos, sin = jnp.cos(ang), jnp.sin(ang)
    l64 = jnp.arange(LANES) % NSA_HEAD_DIM
    idx = l64 % half
    lo, hi = l64 < half, (l64 >= half) & (l64 < ROT_DIM)
    ct = jnp.where(l64 < ROT_DIM, cos[..., idx], 1.0)
    sp = jnp.where(hi, sin[..., idx], 0.0)
    sm = jnp.where(lo, -sin[..., idx], 0.0)
    flat = lambda a: a.reshape(-1, LANES)
    return flat(ct), flat(sp), flat(sm)


def _rope(v, ct, sp, sm):
    half = ROT_DIM // 2
    outs = []
    for c0 in range(0, v.shape[1], LANES):
        vc = v[:, c0:c0 + LANES]
        outs.append(vc * ct + pltpu.roll(vc, half, 1) * sp + pltpu.roll(vc, LANES - half, 1) * sm)
    return outs[0] if len(outs) == 1 else jnp.concatenate(outs, axis=1)


def _proj_kernel(kinds, x_ref, sh_ref, sc_ref, w_ref, ct_ref, sp_ref, sm_ref, o_ref):
    u = (x_ref[...] * (1.0 + sc_ref[0]) + sh_ref[0]).astype(BF16)
    for j, kind in enumerate(kinds):
        acc = jnp.dot(u, w_ref[:, j * GW:(j + 1) * GW], preferred_element_type=F32)
        if kind == ROPE:
            acc = _rope(acc, ct_ref[...], sp_ref[...], sm_ref[...])
        elif kind == SIGMOID:
            acc = jax.nn.sigmoid(acc)
        o_ref[:, j * GW:(j + 1) * GW] = acc.astype(o_ref.dtype)


def nsa_proj(x2, mod3, w_bf, tables, kinds, out_dtype, seq, name, tm=512):
    n, d = x2.shape
    cols = w_bf.shape[1]
    tpb = seq // tm
    assert cols == GW * len(kinds)
    tab = pl.BlockSpec((tm, LANES), lambda i: (i, 0))
    return pl.pallas_call(
        functools.partial(_proj_kernel, kinds),
        grid=(n // tm,),
        in_specs=[
            pl.BlockSpec((tm, d), lambda i: (i, 0)),
            _mod_spec(d, tpb, 0, 1), _mod_spec(d, tpb, 1, 1),
            pl.BlockSpec((d, cols), lambda i: (0, 0)),
            tab, tab, tab,
        ],
        out_specs=pl.BlockSpec((tm, cols), lambda i: (i, 0)),
        out_shape=jax.ShapeDtypeStruct((n, cols), out_dtype),
        compiler_params=_cparams("parallel"),
        name=name,
    )(x2, mod3, mod3, w_bf, *tables)


def _gelu_tanh(v):
    return 0.5 * v * (1.0 + jnp.tanh(0.7978845608028654 * (v + 0.044715 * (v * v * v))))


def _compress_kernel(xl_ref, xh_ref, w1_ref, w2_ref, pa_ref, pb_ref, ct_ref, sp_ref, sm_ref, o_ref):
    nh = o_ref.shape[2]
    a = None
    bm = None
    for p in range(CMP_STRIDE):
        hp = jnp.concatenate([xl_ref[pl.ds(p, nh, stride=CMP_STRIDE), :],
                              xh_ref[pl.ds(p, nh, stride=CMP_STRIDE), :]], axis=1)
        da = jnp.dot((hp + pa_ref[0, p:p + 1, :]).astype(BF16), w1_ref[0, p], preferred_element_type=F32)
        db = jnp.dot((hp + pb_ref[0, p:p + 1, :]).astype(BF16), w1_ref[0, CMP_STRIDE + p],
                     preferred_element_type=F32)
        a = da if a is None else a + da
        bm = db if bm is None else bm + db
    hid = _gelu_tanh(a + pltpu.roll(bm, nh - 1, 0))
    out = jnp.dot(hid.astype(BF16), w2_ref[0], preferred_element_type=F32)

    @pl.when(pl.program_id(1) == 0)
    def _():
        o_ref[0, 0] = _rope(out, ct_ref[0], sp_ref[0], sm_ref[0]).astype(o_ref.dtype)

    @pl.when(pl.program_id(1) != 0)
    def _():
        o_ref[0, 0] = out.astype(o_ref.dtype)


def nsa_compress(aux, w1bd, w2bd, pos_a, pos_b, ctabs, bsz, seq):
    nh = seq // CMP_STRIDE
    hid = w1bd.shape[-1]
    ctab = pl.BlockSpec((1, nh, LANES), lambda b, s: (b, 0, 0))
    return pl.pallas_call(
        _compress_kernel,
        grid=(bsz, 2),
        in_specs=[
            pl.BlockSpec((seq, LANES), lambda b, s: (b, 2 * s)),
            pl.BlockSpec((seq, LANES), lambda b, s: (b, 2 * s + 1)),
            pl.BlockSpec((1, CMP_BLOCK, GW, hid), lambda b, s: (s, 0, 0, 0)),
            pl.BlockSpec((1, hid, GW), lambda b, s: (s, 0, 0)),
            pl.BlockSpec((1, CMP_STRIDE, GW), lambda b, s: (s, 0, 0)),
            pl.BlockSpec((1, CMP_STRIDE, GW), lambda b, s: (s, 0, 0)),
            ctab, ctab, ctab,
        ],
        out_specs=pl.BlockSpec((1, 1, nh, GW), lambda b, s: (s, b, 0, 0)),
        out_shape=jax.ShapeDtypeStruct((2, bsz, nh, GW), BF16),
        compiler_params=_cparams("parallel", "arbitrary"),
        name="nsa_compress",
    )(aux, aux, w1bd, w2bd, pos_a, pos_b, *ctabs)


N_BRANCH = 3


def _gate_rows(gt_ref, branch, tq):
    head0 = pl.program_id(1) * NSA_HEADS_PER_GROUP
    return jnp.concatenate([gt_ref[pl.ds((head0 + r) * N_BRANCH + branch, 1), :]
                            for r in range(NSA_HEADS_PER_GROUP)], axis=1)


def _cmp_sel_kernel(q_ref, kc_ref, vct_ref, g_ref, c2s_ref, oc_ref, sel_ref, qz_ref, gt_ref):
    tq = q_ref.shape[0]
    nc = kc_ref.shape[2]
    r4 = NSA_HEADS_PER_GROUP
    dh = NSA_HEAD_DIM
    g0 = pl.multiple_of(pl.program_id(1) * dh, dh)
    q0 = pl.program_id(2) * tq
    scale = dh ** -0.5
    q_t = (q_ref[...].astype(F32) * scale).T.astype(BF16)
    qz_ref[...] = jnp.zeros(qz_ref.shape, BF16)
    for h in range(r4):
        qz_ref[pl.ds(g0, dh), h * tq:(h + 1) * tq] = q_t[h * dh:(h + 1) * dh, :]
    kc = kc_ref[0, 0]
    vct = vct_ref[0, pl.ds(g0, dh), :]
    t_lanes = q0 + lax.broadcasted_iota(jnp.int32, (1, tq), 1)
    cend = lax.broadcasted_iota(jnp.int32, (nc, 1), 0) * CMP_STRIDE + (CMP_BLOCK - 1)
    ok = cend <= t_lanes
    scores = [jnp.dot(kc, qz_ref[:, h * tq:(h + 1) * tq], preferred_element_type=F32) for h in range(r4)]
    probs = []
    psum = None
    for st in scores:
        st = jnp.where(ok, st, NEG_INF)
        et = jnp.exp(st - jnp.max(st, axis=0, keepdims=True))
        pt = jnp.where(ok, et * (1.0 / jnp.sum(et, axis=0, keepdims=True)), 0.0)
        probs.append(pt.astype(BF16))
        psum = pt if psum is None else psum + pt
    oc_t = [jnp.dot(vct, pt, preferred_element_type=F32) for pt in probs]
    gt_ref[...] = g_ref[:, 0:LANES].T
    gate = _gate_rows(gt_ref, 0, tq)
    oc_ref[...] = jnp.concatenate([oc_t[h] * gate[:, h * tq:(h + 1) * tq] for h in range(r4)], axis=0).T

    imp = jnp.dot(c2s_ref[...], psum, precision=HIGHEST, preferred_element_type=F32)
    ns = imp.shape[0]
    jblk = lax.broadcasted_iota(jnp.int32, (ns, 1), 0)
    cur = t_lanes // SLC_BLOCK
    forced = (jblk == 0) | (jblk == cur) | (jblk == cur - 1)
    valid = jblk <= cur
    score = jnp.where(forced, SEL_BIG, jnp.where(valid, imp, -SEL_BIG))
    nb = ns // SUBLANES
    blocks = [score[b * SUBLANES:(b + 1) * SUBLANES, :] for b in range(nb)]
    jrow = lax.broadcasted_iota(jnp.int32, (SUBLANES, 1), 0)
    cnt = [jnp.zeros((SUBLANES, tq), F32) for _ in range(nb)]
    for i in range(ns):
        bi = i // SUBLANES
        si = score[i:i + 1, :]
        for b in range(nb):
            if b < bi:
                ahead = si > blocks[b]
            elif b > bi:
                ahead = si >= blocks[b]
            else:
                ahead = (si > blocks[b]) | ((si == blocks[b]) & (jrow > i - bi * SUBLANES))
            cnt[b] = cnt[b] + jnp.where(ahead, 1.0, 0.0)
    top_n = min(SLC_TOP_N, ns)
    sel_ref[0, 0] = jnp.where((jnp.concatenate(cnt, axis=0) < top_n) & valid, 1.0, 0.0).astype(sel_ref.dtype)


def nsa_cmp_sel(qkv, kvcmp, aux, c2s_t, bsz, seq, tq=256):
    n = qkv.shape[0]
    g = NSA_KV_GROUPS
    nq = seq // tq
    nc = kvcmp.shape[2]
    ns = seq // SLC_BLOCK
    row = lambda b, gi, i: b * nq + i
    vc_t = jnp.swapaxes(kvcmp[1], 1, 2)
    return pl.pallas_call(
        _cmp_sel_kernel,
        grid=(bsz, g, nq),
        in_specs=[
            pl.BlockSpec((tq, GW), lambda b, gi, i: (row(b, gi, i), gi)),
            pl.BlockSpec((1, 1, nc, GW), lambda b, gi, i: (0, b, 0, 0)),
            pl.BlockSpec((1, GW, nc), lambda b, gi, i: (b, 0, 0)),
            pl.BlockSpec((tq, GW), lambda b, gi, i: (row(b, gi, i), 2)),
            pl.BlockSpec((ns, nc), lambda b, gi, i: (0, 0)),
        ],
        out_specs=[
            pl.BlockSpec((tq, GW), lambda b, gi, i: (row(b, gi, i), gi)),
            pl.BlockSpec((1, 1, ns, tq), lambda b, gi, i: (b, gi, 0, i)),
        ],
        out_shape=[
            jax.ShapeDtypeStruct((n, g * GW), F32),
            jax.ShapeDtypeStruct((bsz, g, ns, seq), BF16),
        ],
        scratch_shapes=[pltpu.VMEM((GW, NSA_HEADS_PER_GROUP * tq), BF16), pltpu.VMEM((LANES, tq), F32)],
        compiler_params=_cparams("parallel", "parallel", "parallel"),
        name="nsa_cmp_sel",
    )(qkv, kvcmp, vc_t, aux, c2s_t)


M_FLOOR = -1e29


ATTN_COLS = 128


def _softmax_tile_t(s_ref, b_ref, p_ref, m, l):
    tq = b_ref.shape[1]
    m_out, l_out, a_out = [], [], []
    for c0 in range(0, s_ref.shape[1], ATTN_COLS):
        b0 = c0 % tq
        sc = s_ref[:, c0:c0 + ATTN_COLS] + b_ref[:, b0:b0 + ATTN_COLS]
        m_old = m[:, c0:c0 + ATTN_COLS]
        m_new = jnp.maximum(m_old, jnp.max(sc, axis=0, keepdims=True))
        a = jnp.exp2(m_old - m_new)
        p = jnp.exp2(sc - m_new)
        l_out.append(a * l[:, c0:c0 + ATTN_COLS] + jnp.sum(p, axis=0, keepdims=True))
        m_out.append(m_new)
        a_out.append(a)
        p_ref[:, c0:c0 + ATTN_COLS] = p.astype(BF16)
    cat = lambda parts: jnp.concatenate(parts, axis=1)
    return cat(m_out), cat(l_out), cat(a_out)


def _flash_branch_t(kt_lo, n_kt, scores, values, bufs):
    dh = NSA_HEAD_DIM
    (s_x, b_x, p_x), (s_y, b_y, p_y) = bufs
    width = s_x.shape[1]
    last = n_kt - 1

    def half_step(kt, cur, nxt, state):
        a_prev, m, l, acc = state
        scores(kt + 1, nxt[0], nxt[1])
        acc = a_prev * acc + jnp.dot(values(jnp.clip(kt - 1, kt_lo, last)), nxt[2][...],
                                     preferred_element_type=F32)
        m, l, a_cur = _softmax_tile_t(cur[0], cur[1], cur[2], m, l)
        return a_cur, m, l, acc

    def body(j, state):
        kt = kt_lo + 2 * j
        state = half_step(kt, (s_x, b_x, p_x), (s_y, b_y, p_y), state)
        return half_step(kt + 1, (s_y, b_y, p_y), (s_x, b_x, p_x), state)

    scores(kt_lo, s_x, b_x)
    p_y[...] = jnp.zeros(p_y.shape, BF16)
    init = (jnp.ones((1, width), F32), jnp.full((1, width), M_FLOOR, F32), jnp.zeros((1, width), F32),
            jnp.zeros((dh, width), F32))
    trips = (n_kt - kt_lo + 1) // 2
    a_last, _, l, acc = lax.fori_loop(0, trips, body, init)
    kt_end = kt_lo + 2 * trips - 1
    acc = a_last * acc + jnp.dot(values(jnp.minimum(kt_end, last)), p_y[...], preferred_element_type=F32)
    return acc / l


def _flash_fixed_t(n_tiles, scores, values, bufs):
    dh = NSA_HEAD_DIM
    width = bufs[0][0].shape[1]
    scores(0, bufs[0][0], bufs[0][1])
    a_prev = m = l = acc = None
    for j in range(n_tiles):
        cur, nxt = bufs[j % 2], bufs[(j + 1) % 2]
        if j + 1 < n_tiles:
            scores(j + 1, nxt[0], nxt[1])
        if j > 0:
            pv = jnp.dot(values(j - 1), nxt[2][...], preferred_element_type=F32)
            acc = pv if acc is None else a_prev * acc + pv
        if m is None:
            m, l = jnp.full((1, width), M_FLOOR, F32), jnp.zeros((1, width), F32)
        m, l, a_prev = _softmax_tile_t(cur[0], cur[1], cur[2], m, l)
    pv = jnp.dot(values(n_tiles - 1), bufs[(n_tiles - 1) % 2][2][...], preferred_element_type=F32)
    acc = pv if acc is None else a_prev * acc + pv
    return acc / l


LOG2E = 1.4426950408889634
MASK_NONE, MASK_LOWER, MASK_UPPER, MASK_ALL = 0, 1, 2, 3
WINDOW_MASKS = (MASK_UPPER, MASK_NONE, MASK_LOWER)


def _attn_kernel(tk, q_ref, ks_ref, vst_ref, kw_ref, vwt_ref, sel_ref, e_ref, cm_ref, g_ref,
                 oc_ref, o_ref, qz_ref, gt_ref, sx_ref, bx_ref, px_ref, sy_ref, by_ref, py_ref):
    tq = q_ref.shape[0]
    r4 = NSA_HEADS_PER_GROUP
    dh = NSA_HEAD_DIM
    g0 = pl.multiple_of(pl.program_id(1) * dh, dh)
    q0 = pl.program_id(2) * tq
    scale = dh ** -0.5 * LOG2E
    q_t = (q_ref[...].astype(F32) * scale).T.astype(BF16)
    qz_ref[...] = jnp.zeros(qz_ref.shape, BF16)
    for h in range(r4):
        qz_ref[pl.ds(g0, dh), h * tq:(h + 1) * tq] = q_t[h * dh:(h + 1) * dh, :]
    sel_bias = ((sel_ref[0, 0].astype(F32) - 1.0) * -NEG_INF).astype(BF16)
    n_kt = pl.program_id(2) + 1

    def tile_scores(k_ref, kt, s_out):
        k0 = pl.multiple_of(jnp.minimum(kt, n_kt - 1) * tk, tk)
        s_out[...] = jnp.dot(k_ref[pl.ds(k0, tk), :], qz_ref[...], preferred_element_type=F32)

    def sel_scores(kt, s_out, b_out):
        tile_scores(ks_ref, kt, s_out)
        which = jnp.where(kt >= n_kt, MASK_ALL, jnp.where(kt == n_kt - 1, MASK_LOWER, MASK_NONE))
        b_out[...] = jnp.dot(e_ref[jnp.minimum(kt, n_kt - 1)], sel_bias,
                             preferred_element_type=F32) + cm_ref[which]

    n_win = len(WINDOW_MASKS)
    win_kt = lambda j: n_kt - n_win + j

    def win_scores(j, s_out, b_out):
        kt = win_kt(j)
        tile_scores(kw_ref, jnp.maximum(kt, 0), s_out)
        b_out[...] = cm_ref[jnp.where(kt >= 0, WINDOW_MASKS[j], MASK_ALL)]

    bufs = ((sx_ref, bx_ref, px_ref), (sy_ref, by_ref, py_ref))
    o_s = _flash_branch_t(0, n_kt, sel_scores, lambda kt: vst_ref[0, kt, pl.ds(g0, dh), :], bufs)
    o_w = _flash_fixed_t(n_win, win_scores,
                         lambda j: vwt_ref[0, jnp.maximum(win_kt(j), 0), pl.ds(g0, dh), :], bufs)

    gt_ref[...] = g_ref[:, 0:LANES].T
    o_t = _gate_rows(gt_ref, 1, tq) * o_s + _gate_rows(gt_ref, 2, tq) * o_w
    o = jnp.concatenate([o_t[:, h * tq:(h + 1) * tq] for h in range(r4)], axis=0).T
    o_ref[...] = (oc_ref[...] + o).astype(o_ref.dtype)


def nsa_attn(qkv, aux, sel_t, oc, e3t, bsz, seq, tq=256, tk=256):
    n = qkv.shape[0]
    g = NSA_KV_GROUPS
    r4 = NSA_HEADS_PER_GROUP
    nq = seq // tq
    nkt = seq // tk
    ns = seq // SLC_BLOCK
    nqt = NSA_HEADS * NSA_HEAD_DIM // GW
    assert tq == tk and WINDOW == 2 * tk, "the constant mask table assumes square tiles and a 2-tile window"
    row = lambda b, gi, i: b * nq + i
    kv = lambda c: pl.BlockSpec((seq, GW), lambda b, gi, i: (b, nqt + c))
    lower = jnp.arange(tk)[:, None] <= jnp.arange(tq)[None, :]
    keep = jnp.stack([jnp.ones_like(lower), lower, ~lower, jnp.zeros_like(lower)])
    cmask = jnp.where(keep, 0.0, NEG_INF).astype(F32)
    vt = lambda c: qkv[:, (nqt + c) * GW:(nqt + c + 1) * GW].reshape(bsz, nkt, tk, GW).transpose(0, 1, 3, 2)
    vt_spec = pl.BlockSpec((1, nkt, GW, tk), lambda b, gi, i: (b, 0, 0, 0))
    return pl.pallas_call(
        functools.partial(_attn_kernel, tk),
        grid=(bsz, g, nq),
        in_specs=[
            pl.BlockSpec((tq, GW), lambda b, gi, i: (row(b, gi, i), gi)),
            kv(0), vt_spec, kv(2), vt_spec,
            pl.BlockSpec((1, 1, ns, tq), lambda b, gi, i: (b, gi, 0, i)),
            pl.BlockSpec((nkt, tk, ns), lambda b, gi, i: (0, 0, 0)),
            pl.BlockSpec((4, tk, tq), lambda b, gi, i: (0, 0, 0)),
            pl.BlockSpec((tq, GW), lambda b, gi, i: (row(b, gi, i), 2)),
            pl.BlockSpec((tq, GW), lambda b, gi, i: (row(b, gi, i), gi)),
        ],
        out_specs=pl.BlockSpec((tq, GW), lambda b, gi, i: (row(b, gi, i), gi)),
        out_shape=jax.ShapeDtypeStruct((n, g * GW), BF16),
        scratch_shapes=[pltpu.VMEM((GW, r4 * tq), BF16), pltpu.VMEM((LANES, tq), F32)] + 2 * [
            pltpu.VMEM((tk, r4 * tq), F32),
            pltpu.VMEM((tk, tq), F32),
            pltpu.VMEM((tk, r4 * tq), BF16),
        ],
        compiler_params=_cparams("parallel", "parallel", "parallel"),
        name="nsa_attn",
    )(qkv, qkv, vt(1), qkv, vt(3), sel_t, e3t, cmask, aux, oc)


def _out_proj_kernel(alpha, a_ref, x_ref, gm_ref, w_ref, lg_ref, lb_ref, o_ref):
    y = jnp.dot(a_ref[...], w_ref[...], preferred_element_type=F32)
    o_ref[...] = _layer_norm(alpha * x_ref[...] + (1.0 + gm_ref[0]) * y, lg_ref[...], lb_ref[...])


def out_proj_ln(a_bf, x2, mod3, chunk, w_bf, ln_g, ln_b, seq, alpha, tm=512):
    n, d = x2.shape
    k = a_bf.shape[1]
    tpb = seq // tm
    return pl.pallas_call(
        functools.partial(_out_proj_kernel, alpha),
        grid=(n // tm,),
        in_specs=[
            pl.BlockSpec((tm, k), lambda i: (i, 0)),
            pl.BlockSpec((tm, d), lambda i: (i, 0)),
            _mod_spec(d, tpb, chunk, 1),
            pl.BlockSpec((k, d), lambda i: (0, 0)),
            pl.BlockSpec((1, d), lambda i: (0, 0)),
            pl.BlockSpec((1, d), lambda i: (0, 0)),
        ],
        out_specs=pl.BlockSpec((tm, d), lambda i: (i, 0)),
        out_shape=jax.ShapeDtypeStruct((n, d), F32),
        compiler_params=_cparams("parallel"),
        name="out_proj_ln",
    )(a_bf, x2, mod3, w_bf, _row(ln_g), _row(ln_b))


def _nsa_constants(seq, tk):
    nc = seq // CMP_STRIDE
    ns = seq // SLC_BLOCK
    sub = SLC_BLOCK // CMP_STRIDE
    seg = jnp.arange(nc)
    c2s_t = sum(((seg[None, :] + o) // sub == jnp.arange(ns)[:, None]).astype(F32)
                for o in range(CMP_BLOCK // CMP_STRIDE))
    key = jnp.arange(seq).reshape(seq // tk, tk, 1)
    e3 = (key // SLC_BLOCK == jnp.arange(ns)[None, None, :])
    return c2s_t, e3.astype(BF16)


def nsa_layer(x2, mod3, positions, w_in, cmp_pos, cmp_w1, cmp_w2, w_out, ln_g, ln_b, seq, alpha,
              tq=256, tk=256):
    n, d = x2.shape
    bsz = n // seq
    g, dh = NSA_KV_GROUPS, NSA_HEAD_DIM
    qw, kvw = NSA_HEADS * dh, g * dh
    cols = lambda k: w_in[:, qw + k * kvw: qw + (k + 1) * kvw]
    gl = w_in[:, qw + 6 * kvw:]
    w_main = jnp.concatenate([w_in[:, :qw], cols(2), cols(3), cols(4), cols(5)], axis=1).astype(BF16)
    w_aux = jnp.concatenate([cols(0), cols(1), jnp.pad(gl, ((0, 0), (0, GW - gl.shape[1])))],
                            axis=1).astype(BF16)
    tables = _rope_tables(positions)
    nqt = qw // GW
    kinds_main = (ROPE,) * nqt + (ROPE, PLAIN, ROPE, PLAIN)
    qkv = nsa_proj(x2, mod3, w_main, tables, kinds_main, BF16, seq, "nsa_proj_main")
    aux = nsa_proj(x2, mod3, w_aux, tables, (PLAIN, PLAIN, SIGMOID), F32, seq, "nsa_proj_aux")

    hid = cmp_w1.shape[-1]
    eye = jnp.eye(g, dtype=F32)
    w1bd = jnp.einsum('spdh,gk->spgdkh', cmp_w1.reshape(2, CMP_BLOCK, dh, hid), eye)
    w1bd = w1bd.reshape(2, CMP_BLOCK, g * dh, g * hid).astype(BF16)
    w2bd = jnp.einsum('shd,gk->sghkd', cmp_w2, eye).reshape(2, g * hid, g * dh).astype(BF16)
    pos_t = jnp.tile(cmp_pos, (1, 1, g))
    nh = seq // CMP_STRIDE
    cmp_positions = jnp.pad(positions[:, CMP_BLOCK - 1::CMP_STRIDE], ((0, 0), (0, 1)))[:, :nh]
    ctabs = tuple(t.reshape(bsz, nh, LANES) for t in _rope_tables(cmp_positions))
    kvcmp = nsa_compress(aux, w1bd, w2bd, pos_t[:, :CMP_STRIDE], pos_t[:, CMP_STRIDE:], ctabs, bsz, seq)

    c2s_t, e3 = _nsa_constants(seq, tk)
    oc, sel_t = nsa_cmp_sel(qkv, kvcmp, aux, c2s_t, bsz, seq, tq)
    o = nsa_attn(qkv, aux, sel_t, oc, e3, bsz, seq, tq, tk)
    return out_proj_ln(o, x2, mod3, 2, w_out.astype(BF16), ln_g, ln_b, seq, alpha)


MOE_ROWS = 512
RINFO = 8
DMA_UNROLL = 8


def _router_kernel(x_ref, sh_ref, sc_ref, r_ref, tril_ref, info_ref, cnt_ref, carry_ref):
    tm = x_ref.shape[0]
    i = pl.program_id(0)

    @pl.when(i == 0)
    def _():
        carry_ref[...] = jnp.zeros_like(carry_ref)

    u = x_ref[...] * (1.0 + sc_ref[0]) + sh_ref[0]
    logits = jnp.dot(u, r_ref[...], precision=HIGHEST, preferred_element_type=F32)
    lane = lax.broadcasted_iota(jnp.int32, (1, LANES), 1)
    logits = jnp.where(lane < N_EXPERTS, logits, -jnp.inf)
    m1 = jnp.max(logits, axis=-1, keepdims=True)
    e1 = jnp.min(jnp.where(logits == m1, lane, LANES), axis=-1, keepdims=True)
    rest = jnp.where(lane == e1, -jnp.inf, logits)
    m2 = jnp.max(rest, axis=-1, keepdims=True)
    e2 = jnp.min(jnp.where(rest == m2, lane, LANES), axis=-1, keepdims=True)
    ex = jnp.exp(m2 - m1)
    g1 = 1.0 / (1.0 + ex)
    g2 = ex / (1.0 + ex)
    o1 = lane == e1
    o2 = lane == e2
    tot = jnp.where(o1 | o2, 1.0, 0.0)
    before = carry_ref[...] + jnp.dot(tril_ref[...], tot.astype(BF16), preferred_element_type=F32)
    rank1 = jnp.sum(jnp.where(o1, before, 0.0), axis=-1, keepdims=True)
    rank2 = jnp.sum(jnp.where(o2, before, 0.0), axis=-1, keepdims=True)
    carry_ref[...] += jnp.sum(tot, axis=0, keepdims=True)
    cnt_ref[...] = carry_ref[...]
    li = lax.broadcasted_iota(jnp.int32, (1, RINFO), 1)
    info = jnp.zeros((tm, RINFO), F32)
    for k, v in enumerate((e1.astype(F32), e2.astype(F32), g1, g2, rank1, rank2)):
        info = jnp.where(li == k, v, info)
    info_ref[...] = info


def moe_router(x2, mod3, router, seq, tm=512):
    n, d = x2.shape
    tpb = seq // tm
    r_pad = jnp.pad(router, ((0, 0), (0, LANES - router.shape[1])))
    tril = (jnp.arange(tm)[:, None] > jnp.arange(tm)[None, :]).astype(BF16)
    return pl.pallas_call(
        _router_kernel,
        grid=(n // tm,),
        in_specs=[
            pl.BlockSpec((tm, d), lambda i: (i, 0)),
            _mod_spec(d, tpb, 3, 1), _mod_spec(d, tpb, 4, 1),
            pl.BlockSpec((d, LANES), lambda i: (0, 0)),
            pl.BlockSpec((tm, tm), lambda i: (0, 0)),
        ],
        out_specs=[
            pl.BlockSpec((tm, RINFO), lambda i: (i, 0)),
            pl.BlockSpec((1, LANES), lambda i: (0, 0)),
        ],
        out_shape=[
            jax.ShapeDtypeStruct((n, RINFO), F32),
            jax.ShapeDtypeStruct((1, LANES), F32),
        ],
        scratch_shapes=[pltpu.VMEM((1, LANES), F32)],
        compiler_params=_cparams("arbitrary"),
        name="moe_router",
    )(x2, mod3, mod3, r_pad, tril)


def _dispatch_kernel(dest_ref, x_ref, sh_ref, sc_ref, xs_in_ref, xs_ref, u_ref, sem):
    del xs_in_ref
    tm = x_ref.shape[0]
    u_ref[...] = x_ref[...] * (1.0 + sc_ref[0]) + sh_ref[0]

    def row_copy(r, k):
        return pltpu.make_async_copy(u_ref.at[pl.ds(r, 1)], xs_ref.at[pl.ds(dest_ref[TOP_K * r + k], 1)], sem)

    def issue(r, carry):
        for k in range(TOP_K):
            row_copy(r, k).start()
        return carry

    def drain(r, carry):
        for k in range(TOP_K):
            row_copy(r, k).wait()
        return carry

    lax.fori_loop(0, tm, issue, 0, unroll=DMA_UNROLL)
    lax.fori_loop(0, tm, drain, 0, unroll=DMA_UNROLL)


def moe_dispatch(x2, mod3, dest, cap, seq, tm=256):
    n, d = x2.shape
    tpb = seq // tm
    xs0 = jnp.zeros((cap, d), F32)
    return pl.pallas_call(
        _dispatch_kernel,
        grid=(n // tm,),
        in_specs=[
            pl.BlockSpec((TOP_K * tm,), lambda i: (i,), memory_space=pltpu.SMEM),
            pl.BlockSpec((tm, d), lambda i: (i, 0)),
            _mod_spec(d, tpb, 3, 1), _mod_spec(d, tpb, 4, 1),
            pl.BlockSpec(memory_space=pl.ANY),
        ],
        out_specs=pl.BlockSpec(memory_space=pl.ANY),
        out_shape=jax.ShapeDtypeStruct((cap, d), F32),
        scratch_shapes=[pltpu.VMEM((tm, d), F32), pltpu.SemaphoreType.DMA],
        input_output_aliases={4: 0},
        compiler_params=_cparams("arbitrary"),
        name="moe_dispatch",
    )(dest, x2, mod3, mod3, xs0)


def _moe_ffn_kernel(bexp_ref, nused_ref, xs_ref, w1_ref, w3_ref, w2_ref, o_ref, xb_ref):
    del bexp_ref
    f = pl.program_id(1)
    used = pl.program_id(0) < nused_ref[0]

    @pl.when(f == 0)
    def _():
        o_ref[...] = jnp.zeros(o_ref.shape, o_ref.dtype)

    @pl.when(used & (f == 0))
    def _():
        xb_ref[...] = xs_ref[...].astype(BF16)

    @pl.when(used)
    def _():
        _swiglu_accumulate(xb_ref, w1_ref[0, 0].astype(BF16), w3_ref[0, 0].astype(BF16),
                           w2_ref[0, 0].astype(BF16), o_ref)


def moe_ffn(xs, blk_exp, n_used, w1, w3, w2, layer, tf=512):
    cap, d = xs.shape
    ff = w1.shape[3]
    grid_spec = pltpu.PrefetchScalarGridSpec(
        num_scalar_prefetch=2,
        grid=(cap // MOE_ROWS, ff // tf),
        in_specs=[
            pl.BlockSpec((MOE_ROWS, d), lambda i, f, be, nu: (i, 0)),
            pl.BlockSpec((1, 1, d, tf), lambda i, f, be, nu: (layer, be[i], 0, f)),
            pl.BlockSpec((1, 1, d, tf), lambda i, f, be, nu: (layer, be[i], 0, f)),
            pl.BlockSpec((1, 1, tf, d), lambda i, f, be, nu: (layer, be[i], f, 0)),
        ],
        out_specs=pl.BlockSpec((MOE_ROWS, d), lambda i, f, be, nu: (i, 0)),
        scratch_shapes=[pltpu.VMEM((MOE_ROWS, d), BF16)],
    )
    return pl.pallas_call(
        _moe_ffn_kernel,
        grid_spec=grid_spec,
        out_shape=jax.ShapeDtypeStruct((cap, d), F32),
        compiler_params=_cparams("parallel", "arbitrary"),
        name="moe_ffn",
    )(blk_exp, n_used, xs, w1, w3, w2)


def _combine_kernel(alpha, dest_ref, info_ref, x_ref, gf_ref, ys_ref, lg_ref, lb_ref, o_ref, buf_ref, sem):
    tm = x_ref.shape[0]

    def row_copy(r, k):
        return pltpu.make_async_copy(ys_ref.at[pl.ds(dest_ref[TOP_K * r + k], 1)],
                                     buf_ref.at[k, pl.ds(r, 1)], sem)

    def issue(r, carry):
        for k in range(TOP_K):
            row_copy(r, k).start()
        return carry

    def drain(r, carry):
        for k in range(TOP_K):
            row_copy(r, k).wait()
        return carry

    lax.fori_loop(0, tm, issue, 0, unroll=DMA_UNROLL)
    lax.fori_loop(0, tm, drain, 0, unroll=DMA_UNROLL)
    info = info_ref[...]
    y = info[:, 2:3] * buf_ref[0] + info[:, 3:4] * buf_ref[1]
    o_ref[...] = _layer_norm(alpha * x_ref[...] + (1.0 + gf_ref[0]) * y, lg_ref[...], lb_ref[...])


def moe_combine(ys, dest, info, x2, mod3, ln_g, ln_b, seq, alpha, tm=256):
    n, d = x2.shape
    tpb = seq // tm
    return pl.pallas_call(
        functools.partial(_combine_kernel, alpha),
        grid=(n // tm,),
        in_specs=[
            pl.BlockSpec((TOP_K * tm,), lambda i: (i,), memory_space=pltpu.SMEM),
            pl.BlockSpec((tm, RINFO), lambda i: (i, 0)),
            pl.BlockSpec((tm, d), lambda i: (i, 0)),
            _mod_spec(d, tpb, 5, 1),
            pl.BlockSpec(memory_space=pl.ANY),
            pl.BlockSpec((1, d), lambda i: (0, 0)),
            pl.BlockSpec((1, d), lambda i: (0, 0)),
        ],
        out_specs=pl.BlockSpec((tm, d), lambda i: (i, 0)),
        out_shape=jax.ShapeDtypeStruct((n, d), F32),
        scratch_shapes=[pltpu.VMEM((TOP_K, tm, d), F32), pltpu.SemaphoreType.DMA],
        compiler_params=_cparams("arbitrary"),
        name="moe_combine",
    )(dest, info, x2, mod3, ys, _row(ln_g), _row(ln_b))


def moe_layer(x2, mod3, router, w1, w3, w2, layer, ln_g, ln_b, seq, alpha):
    n, d = x2.shape
    info, counts = moe_router(x2, mod3, router, seq)
    sizes = counts[0, :N_EXPERTS].astype(jnp.int32)
    padded = (sizes + MOE_ROWS - 1) // MOE_ROWS * MOE_ROWS
    pend = jnp.cumsum(padded)
    pstart = pend - padded
    cap = n * TOP_K + N_EXPERTS * MOE_ROWS
    n_blk = cap // MOE_ROWS
    blk_lo = jnp.arange(n_blk, dtype=jnp.int32) * MOE_ROWS
    blk_exp = jnp.minimum(jnp.sum(blk_lo[:, None] >= pend[None, :], axis=1), N_EXPERTS - 1).astype(jnp.int32)
    n_used = (pend[-1:] // MOE_ROWS).astype(jnp.int32)
    ids = jnp.arange(N_EXPERTS, dtype=jnp.int32)
    e = info[:, 0:TOP_K].astype(jnp.int32)
    start = jnp.sum(jnp.where(e[:, :, None] == ids, pstart, 0), axis=-1)
    dest = (start + info[:, 4:4 + TOP_K].astype(jnp.int32)).reshape(-1)
    xs = moe_dispatch(x2, mod3, dest, cap, seq)
    ys = moe_ffn(xs, blk_exp, n_used, w1, w3, w2, layer)
    return moe_combine(ys, dest, info, x2, mod3, ln_g, ln_b, seq, alpha)


def kernel(x, c, positions, ada_w, ada_b, ln_g, ln_b, conv_w_in, conv_b_in, conv_dw, conv_dw_b, conv_ln_g, conv_ln_b, conv_w_out, conv_b_out, nsa_w_in, nsa_cmp_pos, nsa_cmp_w1, nsa_cmp_w2, nsa_w_out, pool_w, pool_scale, ffn_w1, ffn_w3, ffn_w2, moe_router, moe_w1, moe_w3, moe_w2):
    bsz, seq, d = x.shape
    depth = ada_w.shape[0]
    alpha = (2 * depth) ** 0.25
    mod = ada_mod(c, ada_w, ada_b)
    x2 = x.reshape(bsz * seq, d)
    mods = mod.reshape(depth, bsz, 1, 6 * d)
    for i in range(depth):
        mod3 = mods[i]
        j = i // N_MIXERS
        if i % N_MIXERS == 0:
            h = conv_in(x2, mod3, conv_w_in[j].astype(BF16), conv_b_in[j], seq)
            x2 = conv_tail(h, x2, mod3, conv_dw[j], conv_dw_b[j], conv_ln_g[j], conv_ln_b[j],
                           conv_w_out[j].astype(BF16), conv_b_out[j], ln_g[i, 0], ln_b[i, 0], seq, alpha)
        elif i % N_MIXERS == 1:
            x2 = nsa_layer(x2, mod3, positions, nsa_w_in[j], nsa_cmp_pos[j], nsa_cmp_w1[j], nsa_cmp_w2[j],
                           nsa_w_out[j], ln_g[i, 0], ln_b[i, 0], seq, alpha)
        else:
            x2 = pool_layer(x2, mod3, pool_w[j].astype(BF16), pool_scale[j], ln_g[i, 0], ln_b[i, 0],
                            seq, alpha)
        j = i // 2
        if i % 2 == 0:
            x2 = ffn_dense(x2, mod3, ffn_w1, ffn_w3, ffn_w2, j, ln_g[i, 1], ln_b[i, 1], seq, alpha)
        else:
            x2 = moe_layer(x2, mod3, moe_router[j], moe_w1, moe_w3, moe_w2, j,
                           ln_g[i, 1], ln_b[i, 1], seq, alpha)
    return x2.reshape(bsz, seq, d)
```

```python
import functools

import jax
import jax.numpy as jnp
from jax import lax
from jax.experimental import pallas as pl
from jax.experimental.pallas import tpu as pltpu

F32 = jnp.float32
BF16 = jnp.bfloat16
HIGHEST = lax.Precision.HIGHEST

N_MIXERS = 3
CONV_WIDTH = 31
NSA_HEADS = 16
NSA_KV_GROUPS = 4
NSA_HEADS_PER_GROUP = NSA_HEADS // NSA_KV_GROUPS
NSA_HEAD_DIM = 64
CMP_BLOCK = 32
CMP_STRIDE = 16
SLC_BLOCK = 64
SLC_TOP_N = 16
WINDOW = 512
ROPE_THETA = 500000.0
ROT_DIM = NSA_HEAD_DIM // 4
POOL_WINDOWS = (2, 4, 8, 16)
N_EXPERTS = 8
TOP_K = 2
LN_EPS = 1e-5
NEG_INF = -1e30
SEL_BIG = 1e9

LANES = 128
SUBLANES = 8
VMEM_LIMIT = 56 * 1024 * 1024


def _cparams(*sem):
    return pltpu.CompilerParams(dimension_semantics=sem, vmem_limit_bytes=VMEM_LIMIT)


def _layer_norm(v, g, b):
    mu = jnp.mean(v, axis=-1, keepdims=True)
    d = v - mu
    var = jnp.mean(d * d, axis=-1, keepdims=True)
    return d * lax.rsqrt(var + LN_EPS) * g + b


def _silu(v):
    return v * jax.nn.sigmoid(v)


def _row(v):
    return v.reshape(1, -1)


def _ada_kernel(c_ref, w_ref, b_ref, o_ref):
    cond = _silu(c_ref[...])
    o_ref[0] = jnp.dot(cond, w_ref[0], precision=HIGHEST, preferred_element_type=F32) + b_ref[0]


def ada_mod(c, ada_w, ada_b):
    depth, d, d6 = ada_w.shape
    b = c.shape[0]
    tn = d6 // 4
    return pl.pallas_call(
        _ada_kernel,
        grid=(depth, d6 // tn),
        in_specs=[
            pl.BlockSpec((b, d), lambda i, j: (0, 0)),
            pl.BlockSpec((1, d, tn), lambda i, j: (i, 0, j)),
            pl.BlockSpec((1, 1, tn), lambda i, j: (i, 0, j)),
        ],
        out_specs=pl.BlockSpec((1, b, tn), lambda i, j: (i, 0, j)),
        out_shape=jax.ShapeDtypeStruct((depth, b, d6), F32),
        compiler_params=_cparams("parallel", "parallel"),
        name="ada_mod",
    )(c, ada_w, ada_b.reshape(depth, 1, d6))


def _mod_spec(d, tiles_per_batch, chunk, ngrid):
    if ngrid == 1:
        return pl.BlockSpec((1, 1, d), lambda i: (i // tiles_per_batch, 0, chunk))
    return pl.BlockSpec((1, 1, d), lambda i, j: (i // tiles_per_batch, 0, chunk))


CONV_IN_COLS = 512


def _conv_in_kernel(x_ref, sh_ref, sc_ref, w_ref, b_ref, h_ref):
    d = h_ref.shape[1]
    u = (x_ref[...] * (1.0 + sc_ref[0]) + sh_ref[0]).astype(BF16)
    for c0 in range(0, d, CONV_IN_COLS):
        a = jnp.dot(u, w_ref[:, c0:c0 + CONV_IN_COLS], preferred_element_type=F32) + b_ref[:, c0:c0 + CONV_IN_COLS]
        g = jnp.dot(u, w_ref[:, d + c0:d + c0 + CONV_IN_COLS], preferred_element_type=F32) + b_ref[
            :, d + c0:d + c0 + CONV_IN_COLS]
        h_ref[:, c0:c0 + CONV_IN_COLS] = a * jax.nn.sigmoid(g)


def conv_in(x2, mod3, w_in_bf, b_in, seq, tm=512):
    n, d = x2.shape
    tpb = seq // tm
    return pl.pallas_call(
        _conv_in_kernel,
        grid=(n // tm,),
        in_specs=[
            pl.BlockSpec((tm, d), lambda i: (i, 0)),
            _mod_spec(d, tpb, 0, 1),
            _mod_spec(d, tpb, 1, 1),
            pl.BlockSpec((d, 2 * d), lambda i: (0, 0)),
            pl.BlockSpec((1, 2 * d), lambda i: (0, 0)),
        ],
        out_specs=pl.BlockSpec((tm, d), lambda i: (i, 0)),
        out_shape=jax.ShapeDtypeStruct((n, d), F32),
        compiler_params=_cparams("parallel"),
        name="conv_in",
    )(x2, mod3, mod3, w_in_bf, _row(b_in))


CONV_HALO = 32
CONV_ROWS = 16


def _conv_tail_kernel(tpb, alpha, hc_ref, hp_ref, x_ref, gm_ref, dw_ref, dwb_ref, cg_ref, cb_ref,
                      wo_ref, bo_ref, lg_ref, lb_ref, o_ref, win_ref, cv_ref):
    tm = hc_ref.shape[0]
    first = (pl.program_id(0) % tpb) == 0
    win_ref[0, 0:CONV_HALO, :] = jnp.where(first, 0.0, hp_ref[...])
    win_ref[0, CONV_HALO:CONV_HALO + tm, :] = hc_ref[...]
    keep = tm + CONV_HALO - SUBLANES
    for s in range(1, SUBLANES):
        win_ref[s, 0:keep, :] = win_ref[0, s:s + keep, :]
    off = CONV_HALO - (CONV_WIDTH - 1)
    halves = CONV_ROWS // SUBLANES
    for rc in range(tm // CONV_ROWS):
        r0 = rc * CONV_ROWS
        accs = [None] * halves
        for k in range(CONV_WIDTH):
            j = off + k
            a0 = r0 + j - j % SUBLANES
            tap = dw_ref[k]
            for i in range(halves):
                term = win_ref[j % SUBLANES, a0 + i * SUBLANES:a0 + (i + 1) * SUBLANES, :] * tap
                accs[i] = term + dwb_ref[...] if accs[i] is None else accs[i] + term
        for i in range(halves):
            cv_ref[r0 + i * SUBLANES:r0 + (i + 1) * SUBLANES, :] = accs[i]
    hn = _silu(_layer_norm(cv_ref[...], cg_ref[...], cb_ref[...]))
    y = jnp.dot(hn.astype(BF16), wo_ref[...], preferred_element_type=F32) + bo_ref[...]
    o_ref[...] = _layer_norm(alpha * x_ref[...] + (1.0 + gm_ref[0]) * y, lg_ref[...], lb_ref[...])


def conv_tail(h, x2, mod3, dw, dw_b, cln_g, cln_b, w_out_bf, b_out, ln_g, ln_b, seq, alpha, tm=256):
    n, d = x2.shape
    tpb = seq // tm
    hb = tm // CONV_HALO
    full = lambda shape: pl.BlockSpec(shape, lambda i: (0, 0))
    return pl.pallas_call(
        functools.partial(_conv_tail_kernel, tpb, alpha),
        grid=(n // tm,),
        in_specs=[
            pl.BlockSpec((tm, d), lambda i: (i, 0)),
            pl.BlockSpec((CONV_HALO, d), lambda i: (jnp.maximum(i * hb - 1, 0), 0)),
            pl.BlockSpec((tm, d), lambda i: (i, 0)),
            _mod_spec(d, tpb, 2, 1),
            pl.BlockSpec((CONV_WIDTH, SUBLANES, d), lambda i: (0, 0, 0)), full((1, d)), full((1, d)), full((1, d)),
            full((d, d)), full((1, d)), full((1, d)), full((1, d)),
        ],
        out_specs=pl.BlockSpec((tm, d), lambda i: (i, 0)),
        out_shape=jax.ShapeDtypeStruct((n, d), F32),
        scratch_shapes=[pltpu.VMEM((SUBLANES, tm + CONV_HALO, d), F32), pltpu.VMEM((tm, d), F32)],
        compiler_params=_cparams("parallel"),
        name="conv_tail",
    )(h, h, x2, mod3, jnp.broadcast_to(dw[:, None, :], (CONV_WIDTH, SUBLANES, d)), _row(dw_b), _row(cln_g),
      _row(cln_b), w_out_bf, _row(b_out), _row(ln_g), _row(ln_b))


FFN_ROWS = 256


def _swiglu_accumulate(u_ref, w1, w3, w2, acc_ref):
    for r0 in range(0, u_ref.shape[0], FFN_ROWS):
        u = u_ref[r0:r0 + FFN_ROWS, :]
        h = _silu(jnp.dot(u, w1, preferred_element_type=F32)) * jnp.dot(u, w3, preferred_element_type=F32)
        acc_ref[r0:r0 + FFN_ROWS, :] += jnp.dot(h.astype(BF16), w2, preferred_element_type=F32)


def _ffn_kernel(alpha, x_ref, sh_ref, sc_ref, gf_ref, w1_ref, w3_ref, w2_ref, lg_ref, lb_ref,
                o_ref, u_ref, acc_ref):
    f = pl.program_id(1)

    @pl.when(f == 0)
    def _():
        u_ref[...] = (x_ref[...] * (1.0 + sc_ref[0]) + sh_ref[0]).astype(BF16)
        acc_ref[...] = jnp.zeros(acc_ref.shape, F32)

    _swiglu_accumulate(u_ref, w1_ref[0].astype(BF16), w3_ref[0].astype(BF16), w2_ref[0].astype(BF16), acc_ref)

    @pl.when(f == pl.num_programs(1) - 1)
    def _():
        o_ref[...] = _layer_norm(alpha * x_ref[...] + (1.0 + gf_ref[0]) * acc_ref[...],
                                 lg_ref[...], lb_ref[...])


def ffn_dense(x2, mod3, w1, w3, w2, layer, ln_g, ln_b, seq, alpha, tm=1024, tf=512):
    n, d = x2.shape
    ff = w1.shape[2]
    tpb = seq // tm
    return pl.pallas_call(
        functools.partial(_ffn_kernel, alpha),
        grid=(n // tm, ff // tf),
        in_specs=[
            pl.BlockSpec((tm, d), lambda i, f: (i, 0)),
            _mod_spec(d, tpb, 3, 2), _mod_spec(d, tpb, 4, 2), _mod_spec(d, tpb, 5, 2),
            pl.BlockSpec((1, d, tf), lambda i, f: (layer, 0, f)),
            pl.BlockSpec((1, d, tf), lambda i, f: (layer, 0, f)),
            pl.BlockSpec((1, tf, d), lambda i, f: (layer, f, 0)),
            pl.BlockSpec((1, d), lambda i, f: (0, 0)),
            pl.BlockSpec((1, d), lambda i, f: (0, 0)),
        ],
        out_specs=pl.BlockSpec((tm, d), lambda i, f: (i, 0)),
        out_shape=jax.ShapeDtypeStruct((n, d), F32),
        scratch_shapes=[pltpu.VMEM((tm, d), BF16), pltpu.VMEM((tm, d), F32)],
        compiler_params=_cparams("parallel", "arbitrary"),
        name="ffn_dense",
    )(x2, mod3, mod3, mod3, w1, w3, w2, _row(ln_g), _row(ln_b))


POOL_HALO = 16


def _pool_kernel(tpb, alpha, xc_ref, xp_ref, sh_ref, sc_ref, gm_ref, w_ref, ps_ref, lg_ref, lb_ref,
                 o_ref, win_ref):
    tm, d = xc_ref.shape
    gw = d // len(POOL_WINDOWS)
    tile = pl.program_id(0) % tpb
    first = tile == 0
    scale = 1.0 + sc_ref[0]
    up = xp_ref[...] * scale + sh_ref[0]
    uc = xc_ref[...] * scale + sh_ref[0]
    win_ref[0:POOL_HALO, :] = jnp.where(first, 0.0, up)
    win_ref[POOL_HALO:POOL_HALO + tm, :] = uc
    t = tile * tm + lax.broadcasted_iota(jnp.int32, (tm, 1), 0)
    ys = []
    for gi, w in enumerate(POOL_WINDOWS):
        c0 = gi * gw
        s = uc[:, c0:c0 + gw]
        for k in range(1, w):
            s = s + win_ref[POOL_HALO - k:POOL_HALO - k + tm, c0:c0 + gw]
        cnt = jnp.minimum(t + 1, w).astype(F32)
        p = s / cnt - uc[:, c0:c0 + gw]
        ys.append(jnp.dot(p.astype(BF16), w_ref[gi], preferred_element_type=F32))
    y = jnp.concatenate(ys, axis=-1) * ps_ref[...]
    o_ref[...] = _layer_norm(alpha * xc_ref[...] + (1.0 + gm_ref[0]) * y, lg_ref[...], lb_ref[...])


def pool_layer(x2, mod3, w_grp_bf, pool_scale, ln_g, ln_b, seq, alpha, tm=512):
    n, d = x2.shape
    tpb = seq // tm
    hb = tm // POOL_HALO
    ng, gw, _ = w_grp_bf.shape
    return pl.pallas_call(
        functools.partial(_pool_kernel, tpb, alpha),
        grid=(n // tm,),
        in_specs=[
            pl.BlockSpec((tm, d), lambda i: (i, 0)),
            pl.BlockSpec((POOL_HALO, d), lambda i: (jnp.maximum(i * hb - 1, 0), 0)),
            _mod_spec(d, tpb, 0, 1), _mod_spec(d, tpb, 1, 1), _mod_spec(d, tpb, 2, 1),
            pl.BlockSpec((ng, gw, gw), lambda i: (0, 0, 0)),
            pl.BlockSpec((1, d), lambda i: (0, 0)),
            pl.BlockSpec((1, d), lambda i: (0, 0)),
            pl.BlockSpec((1, d), lambda i: (0, 0)),
        ],
        out_specs=pl.BlockSpec((tm, d), lambda i: (i, 0)),
        out_shape=jax.ShapeDtypeStruct((n, d), F32),
        scratch_shapes=[pltpu.VMEM((tm + POOL_HALO, d), F32)],
        compiler_params=_cparams("parallel"),
        name="pool_layer",
    )(x2, x2, mod3, mod3, mod3, w_grp_bf, _row(pool_scale), _row(ln_g), _row(ln_b))


GW = NSA_HEADS_PER_GROUP * NSA_HEAD_DIM
PLAIN, ROPE, SIGMOID = 0, 1, 2


def _rope_tables(positions):
    half = ROT_DIM // 2
    inv = ROPE_THETA ** (-jnp.arange(half, dtype=F32) * 2.0 / ROT_DIM)
    ang = positions.astype(F32)[..., None] * inv
    cos, sin = jnp.cos(ang), jnp.sin(ang)
    l64 = jnp.arange(LANES) % NSA_HEAD_DIM
    idx = l64 % half
    lo, hi = l64 < half, (l64 >= half) & (l64 < ROT_DIM)
    ct = jnp.where(l64 < ROT_DIM, cos[..., idx], 1.0)
    sp = jnp.where(hi, sin[..., idx], 0.0)
    sm = jnp.where(lo, -sin[..., idx], 0.0)
    flat = lambda a: a.reshape(-1, LANES)
    return flat(ct), flat(sp), flat(sm)


def _rope(v, ct, sp, sm):
    half = ROT_DIM // 2
    outs = []
    for c0 in range(0, v.shape[1], LANES):
        vc = v[:, c0:c0 + LANES]
        outs.append(vc * ct + pltpu.roll(vc, half, 1) * sp + pltpu.roll(vc, LANES - half, 1) * sm)
    return outs[0] if len(outs) == 1 else jnp.concatenate(outs, axis=1)


def _proj_kernel(kinds, x_ref, sh_ref, sc_ref, w_ref, ct_ref, sp_ref, sm_ref, o_ref):
    u = (x_ref[...] * (1.0 + sc_ref[0]) + sh_ref[0]).astype(BF16)
    for j, kind in enumerate(kinds):
        acc = jnp.dot(u, w_ref[:, j * GW:(j + 1) * GW], preferred_element_type=F32)
        if kind == ROPE:
            acc = _rope(acc, ct_ref[...], sp_ref[...], sm_ref[...])
        elif kind == SIGMOID:
            acc = jax.nn.sigmoid(acc)
        o_ref[:, j * GW:(j + 1) * GW] = acc.astype(o_ref.dtype)


def nsa_proj(x2, mod3, w_bf, tables, kinds, out_dtype, seq, name, tm=512):
    n, d = x2.shape
    cols = w_bf.shape[1]
    tpb = seq // tm
    assert cols == GW * len(kinds)
    tab = pl.BlockSpec((tm, LANES), lambda i: (i, 0))
    return pl.pallas_call(
        functools.partial(_proj_kernel, kinds),
        grid=(n // tm,),
        in_specs=[
            pl.BlockSpec((tm, d), lambda i: (i, 0)),
            _mod_spec(d, tpb, 0, 1), _mod_spec(d, tpb, 1, 1),
            pl.BlockSpec((d, cols), lambda i: (0, 0)),
            tab, tab, tab,
        ],
        out_specs=pl.BlockSpec((tm, cols), lambda i: (i, 0)),
        out_shape=jax.ShapeDtypeStruct((n, cols), out_dtype),
        compiler_params=_cparams("parallel"),
        name=name,
    )(x2, mod3, mod3, w_bf, *tables)


def _gelu_tanh(v):
    return 0.5 * v * (1.0 + jnp.tanh(0.7978845608028654 * (v + 0.044715 * (v * v * v))))


def _compress_kernel(xl_ref, xh_ref, w1_ref, w2_ref, pa_ref, pb_ref, ct_ref, sp_ref, sm_ref, o_ref):
    nh = o_ref.shape[2]
    a = None
    bm = None
    for p in range(CMP_STRIDE):
        hp = jnp.concatenate([xl_ref[pl.ds(p, nh, stride=CMP_STRIDE), :],
                              xh_ref[pl.ds(p, nh, stride=CMP_STRIDE), :]], axis=1)
        da = jnp.dot((hp + pa_ref[0, p:p + 1, :]).astype(BF16), w1_ref[0, p], preferred_element_type=F32)
        db = jnp.dot((hp + pb_ref[0, p:p + 1, :]).astype(BF16), w1_ref[0, CMP_STRIDE + p],
                     preferred_element_type=F32)
        a = da if a is None else a + da
        bm = db if bm is None else bm + db
    hid = _gelu_tanh(a + pltpu.roll(bm, nh - 1, 0))
    out = jnp.dot(hid.astype(BF16), w2_ref[0], preferred_element_type=F32)

    @pl.when(pl.program_id(1) == 0)
    def _():
        o_ref[0, 0] = _rope(out, ct_ref[0], sp_ref[0], sm_ref[0]).astype(o_ref.dtype)

    @pl.when(pl.program_id(1) != 0)
    def _():
        o_ref[0, 0] = out.astype(o_ref.dtype)


def nsa_compress(aux, w1bd, w2bd, pos_a, pos_b, ctabs, bsz, seq):
    nh = seq // CMP_STRIDE
    hid = w1bd.shape[-1]
    ctab = pl.BlockSpec((1, nh, LANES), lambda b, s: (b, 0, 0))
    return pl.pallas_call(
        _compress_kernel,
        grid=(bsz, 2),
        in_specs=[
            pl.BlockSpec((seq, LANES), lambda b, s: (b, 2 * s)),
            pl.BlockSpec((seq, LANES), lambda b, s: (b, 2 * s + 1)),
            pl.BlockSpec((1, CMP_BLOCK, GW, hid), lambda b, s: (s, 0, 0, 0)),
            pl.BlockSpec((1, hid, GW), lambda b, s: (s, 0, 0)),
            pl.BlockSpec((1, CMP_STRIDE, GW), lambda b, s: (s, 0, 0)),
            pl.BlockSpec((1, CMP_STRIDE, GW), lambda b, s: (s, 0, 0)),
            ctab, ctab, ctab,
        ],
        out_specs=pl.BlockSpec((1, 1, nh, GW), lambda b, s: (s, b, 0, 0)),
        out_shape=jax.ShapeDtypeStruct((2, bsz, nh, GW), BF16),
        compiler_params=_cparams("parallel", "arbitrary"),
        name="nsa_compress",
    )(aux, aux, w1bd, w2bd, pos_a, pos_b, *ctabs)


N_BRANCH = 3


def _gate_rows(gt_ref, branch, tq):
    head0 = pl.program_id(1) * NSA_HEADS_PER_GROUP
    return jnp.concatenate([gt_ref[pl.ds((head0 + r) * N_BRANCH + branch, 1), :]
                            for r in range(NSA_HEADS_PER_GROUP)], axis=1)


def _cmp_sel_kernel(q_ref, kc_ref, vct_ref, g_ref, c2s_ref, oc_ref, sel_ref, qz_ref, gt_ref):
    tq = q_ref.shape[0]
    nc = kc_ref.shape[2]
    r4 = NSA_HEADS_PER_GROUP
    dh = NSA_HEAD_DIM
    g0 = pl.multiple_of(pl.program_id(1) * dh, dh)
    q0 = pl.program_id(2) * tq
    scale = dh ** -0.5
    q_t = (q_ref[...].astype(F32) * scale).T.astype(BF16)
    qz_ref[...] = jnp.zeros(qz_ref.shape, BF16)
    for h in range(r4):
        qz_ref[pl.ds(g0, dh), h * tq:(h + 1) * tq] = q_t[h * dh:(h + 1) * dh, :]
    kc = kc_ref[0, 0]
    vct = vct_ref[0, pl.ds(g0, dh), :]
    t_lanes = q0 + lax.broadcasted_iota(jnp.int32, (1, tq), 1)
    cend = lax.broadcasted_iota(jnp.int32, (nc, 1), 0) * CMP_STRIDE + (CMP_BLOCK - 1)
    ok = cend <= t_lanes
    scores = [jnp.dot(kc, qz_ref[:, h * tq:(h + 1) * tq], preferred_element_type=F32) for h in range(r4)]
    probs = []
    psum = None
    for st in scores:
        st = jnp.where(ok, st, NEG_INF)
        et = jnp.exp(st - jnp.max(st, axis=0, keepdims=True))
        pt = jnp.where(ok, et * (1.0 / jnp.sum(et, axis=0, keepdims=True)), 0.0)
        probs.append(pt.astype(BF16))
        psum = pt if psum is None else psum + pt
    oc_t = [jnp.dot(vct, pt, preferred_element_type=F32) for pt in probs]
    gt_ref[...] = g_ref[:, 0:LANES].T
    gate = _gate_rows(gt_ref, 0, tq)
    oc_ref[...] = jnp.concatenate([oc_t[h] * gate[:, h * tq:(h + 1) * tq] for h in range(r4)], axis=0).T

    imp = jnp.dot(c2s_ref[...], psum, precision=HIGHEST, preferred_element_type=F32)
    ns = imp.shape[0]
    jblk = lax.broadcasted_iota(jnp.int32, (ns, 1), 0)
    cur = t_lanes // SLC_BLOCK
    forced = (jblk == 0) | (jblk == cur) | (jblk == cur - 1)
    valid = jblk <= cur
    score = jnp.where(forced, SEL_BIG, jnp.where(valid, imp, -SEL_BIG))
    nb = ns // SUBLANES
    blocks = [score[b * SUBLANES:(b + 1) * SUBLANES, :] for b in range(nb)]
    jrow = lax.broadcasted_iota(jnp.int32, (SUBLANES, 1), 0)
    cnt = [jnp.zeros((SUBLANES, tq), F32) for _ in range(nb)]
    for i in range(ns):
        bi = i // SUBLANES
        si = score[i:i + 1, :]
        for b in range(nb):
            if b < bi:
                ahead = si > blocks[b]
            elif b > bi:
                ahead = si >= blocks[b]
            else:
                ahead = (si > blocks[b]) | ((si == blocks[b]) & (jrow > i - bi * SUBLANES))
            cnt[b] = cnt[b] + jnp.where(ahead, 1.0, 0.0)
    top_n = min(SLC_TOP_N, ns)
    sel_ref[0, 0] = jnp.where((jnp.concatenate(cnt, axis=0) < top_n) & valid, 1.0, 0.0).astype(sel_ref.dtype)


def nsa_cmp_sel(qkv, kvcmp, aux, c2s_t, bsz, seq, tq=256):
    n = qkv.shape[0]
    g = NSA_KV_GROUPS
    nq = seq // tq
    nc = kvcmp.shape[2]
    ns = seq // SLC_BLOCK
    row = lambda b, gi, i: b * nq + i
    vc_t = jnp.swapaxes(kvcmp[1], 1, 2)
    return pl.pallas_call(
        _cmp_sel_kernel,
        grid=(bsz, g, nq),
        in_specs=[
            pl.BlockSpec((tq, GW), lambda b, gi, i: (row(b, gi, i), gi)),
            pl.BlockSpec((1, 1, nc, GW), lambda b, gi, i: (0, b, 0, 0)),
            pl.BlockSpec((1, GW, nc), lambda b, gi, i: (b, 0, 0)),
            pl.BlockSpec((tq, GW), lambda b, gi, i: (row(b, gi, i), 2)),
            pl.BlockSpec((ns, nc), lambda b, gi, i: (0, 0)),
        ],
        out_specs=[
            pl.BlockSpec((tq, GW), lambda b, gi, i: (row(b, gi, i), gi)),
            pl.BlockSpec((1, 1, ns, tq), lambda b, gi, i: (b, gi, 0, i)),
        ],
        out_shape=[
            jax.ShapeDtypeStruct((n, g * GW), F32),
            jax.ShapeDtypeStruct((bsz, g, ns, seq), BF16),
        ],
        scratch_shapes=[pltpu.VMEM((GW, NSA_HEADS_PER_GROUP * tq), BF16), pltpu.VMEM((LANES, tq), F32)],
        compiler_params=_cparams("parallel", "parallel", "parallel"),
        name="nsa_cmp_sel",
    )(qkv, kvcmp, vc_t, aux, c2s_t)


M_FLOOR = -1e29


ATTN_COLS = 128


def _softmax_tile_t(s_ref, b_ref, p_ref, m, l):
    tq = b_ref.shape[1]
    m_out, l_out, a_out = [], [], []
    for c0 in range(0, s_ref.shape[1], ATTN_COLS):
        b0 = c0 % tq
        sc = s_ref[:, c0:c0 + ATTN_COLS] + b_ref[:, b0:b0 + ATTN_COLS]
        m_old = m[:, c0:c0 + ATTN_COLS]
        m_new = jnp.maximum(m_old, jnp.max(sc, axis=0, keepdims=True))
        a = jnp.exp2(m_old - m_new)
        p = jnp.exp2(sc - m_new)
        l_out.append(a * l[:, c0:c0 + ATTN_COLS] + jnp.sum(p, axis=0, keepdims=True))
        m_out.append(m_new)
        a_out.append(a)
        p_ref[:, c0:c0 + ATTN_COLS] = p.astype(BF16)
    cat = lambda parts: jnp.concatenate(parts, axis=1)
    return cat(m_out), cat(l_out), cat(a_out)


def _flash_branch_t(kt_lo, n_kt, scores, values, bufs):
    dh = NSA_HEAD_DIM
    (s_x, b_x, p_x), (s_y, b_y, p_y) = bufs
    width = s_x.shape[1]
    last = n_kt - 1

    def half_step(kt, cur, nxt, state):
        a_prev, m, l, acc = state
        scores(kt + 1, nxt[0], nxt[1])
        acc = a_prev * acc + jnp.dot(values(jnp.clip(kt - 1, kt_lo, last)), nxt[2][...],
                                     preferred_element_type=F32)
        m, l, a_cur = _softmax_tile_t(cur[0], cur[1], cur[2], m, l)
        return a_cur, m, l, acc

    def body(j, state):
        kt = kt_lo + 2 * j
        state = half_step(kt, (s_x, b_x, p_x), (s_y, b_y, p_y), state)
        return half_step(kt + 1, (s_y, b_y, p_y), (s_x, b_x, p_x), state)

    scores(kt_lo, s_x, b_x)
    p_y[...] = jnp.zeros(p_y.shape, BF16)
    init = (jnp.ones((1, width), F32), jnp.full((1, width), M_FLOOR, F32), jnp.zeros((1, width), F32),
            jnp.zeros((dh, width), F32))
    trips = (n_kt - kt_lo + 1) // 2
    a_last, _, l, acc = lax.fori_loop(0, trips, body, init)
    kt_end = kt_lo + 2 * trips - 1
    acc = a_last * acc + jnp.dot(values(jnp.minimum(kt_end, last)), p_y[...], preferred_element_type=F32)
    return acc / l


def _flash_fixed_t(n_tiles, scores, values, bufs):
    dh = NSA_HEAD_DIM
    width = bufs[0][0].shape[1]
    scores(0, bufs[0][0], bufs[0][1])
    a_prev = m = l = acc = None
    for j in range(n_tiles):
        cur, nxt = bufs[j % 2], bufs[(j + 1) % 2]
        if j + 1 < n_tiles:
            scores(j + 1, nxt[0], nxt[1])
        if j > 0:
            pv = jnp.dot(values(j - 1), nxt[2][...], preferred_element_type=F32)
            acc = pv if acc is None else a_prev * acc + pv
        if m is None:
            m, l = jnp.full((1, width), M_FLOOR, F32), jnp.zeros((1, width), F32)
        m, l, a_prev = _softmax_tile_t(cur[0], cur[1], cur[2], m, l)
    pv = jnp.dot(values(n_tiles - 1), bufs[(n_tiles - 1) % 2][2][...], preferred_element_type=F32)
    acc = pv if acc is None else a_prev * acc + pv
    return acc / l


LOG2E = 1.4426950408889634
MASK_NONE, MASK_LOWER, MASK_UPPER, MASK_ALL = 0, 1, 2, 3


def _window_masks(tk):
    return (MASK_UPPER,) + (MASK_NONE,) * (WINDOW // tk - 1) + (MASK_LOWER,)


def _attn_kernel(tk, q_ref, ks_ref, vst_ref, kw_ref, vwt_ref, sel_ref, e_ref, cm_ref, g_ref,
                 oc_ref, o_ref, qz_ref, gt_ref, sx_ref, bx_ref, px_ref, sy_ref, by_ref, py_ref):
    tq = q_ref.shape[0]
    r4 = NSA_HEADS_PER_GROUP
    dh = NSA_HEAD_DIM
    g0 = pl.multiple_of(pl.program_id(1) * dh, dh)
    q0 = pl.program_id(2) * tq
    scale = dh ** -0.5 * LOG2E
    q_t = (q_ref[...].astype(F32) * scale).T.astype(BF16)
    qz_ref[...] = jnp.zeros(qz_ref.shape, BF16)
    for h in range(r4):
        qz_ref[pl.ds(g0, dh), h * tq:(h + 1) * tq] = q_t[h * dh:(h + 1) * dh, :]
    sel_bias = ((sel_ref[0, 0].astype(F32) - 1.0) * -NEG_INF).astype(BF16)
    n_kt = pl.program_id(2) + 1

    def tile_scores(k_ref, kt, s_out):
        k0 = pl.multiple_of(jnp.minimum(kt, n_kt - 1) * tk, tk)
        s_out[...] = jnp.dot(k_ref[pl.ds(k0, tk), :], qz_ref[...], preferred_element_type=F32)

    def sel_scores(kt, s_out, b_out):
        tile_scores(ks_ref, kt, s_out)
        which = jnp.where(kt >= n_kt, MASK_ALL, jnp.where(kt == n_kt - 1, MASK_LOWER, MASK_NONE))
        b_out[...] = jnp.dot(e_ref[jnp.minimum(kt, n_kt - 1)], sel_bias,
                             preferred_element_type=F32) + cm_ref[which]

    win_masks = _window_masks(tk)
    n_win = len(win_masks)
    win_kt = lambda j: n_kt - n_win + j

    def win_scores(j, s_out, b_out):
        kt = win_kt(j)
        tile_scores(kw_ref, jnp.maximum(kt, 0), s_out)
        b_out[...] = cm_ref[jnp.where(kt >= 0, win_masks[j], MASK_ALL)]

    bufs = ((sx_ref, bx_ref, px_ref), (sy_ref, by_ref, py_ref))
    o_s = _flash_branch_t(0, n_kt, sel_scores, lambda kt: vst_ref[0, kt, pl.ds(g0, dh), :], bufs)
    o_w = _flash_fixed_t(n_win, win_scores,
                         lambda j: vwt_ref[0, jnp.maximum(win_kt(j), 0), pl.ds(g0, dh), :], bufs)

    gt_ref[...] = g_ref[:, 0:LANES].T
    o_t = _gate_rows(gt_ref, 1, tq) * o_s + _gate_rows(gt_ref, 2, tq) * o_w
    o = jnp.concatenate([o_t[:, h * tq:(h + 1) * tq] for h in range(r4)], axis=0).T
    o_ref[...] = (oc_ref[...] + o).astype(o_ref.dtype)


def nsa_attn(qkv, aux, sel_t, oc, e3t, bsz, seq, tq=256, tk=256):
    n = qkv.shape[0]
    g = NSA_KV_GROUPS
    r4 = NSA_HEADS_PER_GROUP
    nq = seq // tq
    nkt = seq // tk
    ns = seq // SLC_BLOCK
    nqt = NSA_HEADS * NSA_HEAD_DIM // GW
    assert tq == tk and WINDOW % tk == 0, "the constant mask table assumes square tiles dividing the window"
    row = lambda b, gi, i: b * nq + i
    kv = lambda c: pl.BlockSpec((seq, GW), lambda b, gi, i: (b, nqt + c))
    lower = jnp.arange(tk)[:, None] <= jnp.arange(tq)[None, :]
    keep = jnp.stack([jnp.ones_like(lower), lower, ~lower, jnp.zeros_like(lower)])
    cmask = jnp.where(keep, 0.0, NEG_INF).astype(F32)
    vt = lambda c: qkv[:, (nqt + c) * GW:(nqt + c + 1) * GW].reshape(bsz, nkt, tk, GW).transpose(0, 1, 3, 2)
    vt_spec = pl.BlockSpec((1, nkt, GW, tk), lambda b, gi, i: (b, 0, 0, 0))
    return pl.pallas_call(
        functools.partial(_attn_kernel, tk),
        grid=(bsz, g, nq),
        in_specs=[
            pl.BlockSpec((tq, GW), lambda b, gi, i: (row(b, gi, i), gi)),
            kv(0), vt_spec, kv(2), vt_spec,
            pl.BlockSpec((1, 1, ns, tq), lambda b, gi, i: (b, gi, 0, i)),
            pl.BlockSpec((nkt, tk, ns), lambda b, gi, i: (0, 0, 0)),
            pl.BlockSpec((4, tk, tq), lambda b, gi, i: (0, 0, 0)),
            pl.BlockSpec((tq, GW), lambda b, gi, i: (row(b, gi, i), 2)),
            pl.BlockSpec((tq, GW), lambda b, gi, i: (row(b, gi, i), gi)),
        ],
        out_specs=pl.BlockSpec((tq, GW), lambda b, gi, i: (row(b, gi, i), gi)),
        out_shape=jax.ShapeDtypeStruct((n, g * GW), BF16),
        scratch_shapes=[pltpu.VMEM((GW, r4 * tq), BF16), pltpu.VMEM((LANES, tq), F32)] + 2 * [
            pltpu.VMEM((tk, r4 * tq), F32),
            pltpu.VMEM((tk, tq), F32),
            pltpu.VMEM((tk, r4 * tq), BF16),
        ],
        compiler_params=_cparams("parallel", "parallel", "parallel"),
        name="nsa_attn",
    )(qkv, qkv, vt(1), qkv, vt(3), sel_t, e3t, cmask, aux, oc)


def _out_proj_kernel(alpha, a_ref, x_ref, gm_ref, w_ref, lg_ref, lb_ref, o_ref):
    y = jnp.dot(a_ref[...], w_ref[...], preferred_element_type=F32)
    o_ref[...] = _layer_norm(alpha * x_ref[...] + (1.0 + gm_ref[0]) * y, lg_ref[...], lb_ref[...])


def out_proj_ln(a_bf, x2, mod3, chunk, w_bf, ln_g, ln_b, seq, alpha, tm=512):
    n, d = x2.shape
    k = a_bf.shape[1]
    tpb = seq // tm
    return pl.pallas_call(
        functools.partial(_out_proj_kernel, alpha),
        grid=(n // tm,),
        in_specs=[
            pl.BlockSpec((tm, k), lambda i: (i, 0)),
            pl.BlockSpec((tm, d), lambda i: (i, 0)),
            _mod_spec(d, tpb, chunk, 1),
            pl.BlockSpec((k, d), lambda i: (0, 0)),
            pl.BlockSpec((1, d), lambda i: (0, 0)),
            pl.BlockSpec((1, d), lambda i: (0, 0)),
        ],
        out_specs=pl.BlockSpec((tm, d), lambda i: (i, 0)),
        out_shape=jax.ShapeDtypeStruct((n, d), F32),
        compiler_params=_cparams("parallel"),
        name="out_proj_ln",
    )(a_bf, x2, mod3, w_bf, _row(ln_g), _row(ln_b))


def _nsa_constants(seq, tk):
    nc = seq // CMP_STRIDE
    ns = seq // SLC_BLOCK
    sub = SLC_BLOCK // CMP_STRIDE
    seg = jnp.arange(nc)
    c2s_t = sum(((seg[None, :] + o) // sub == jnp.arange(ns)[:, None]).astype(F32)
                for o in range(CMP_BLOCK // CMP_STRIDE))
    key = jnp.arange(seq).reshape(seq // tk, tk, 1)
    e3 = (key // SLC_BLOCK == jnp.arange(ns)[None, None, :])
    return c2s_t, e3.astype(BF16)


def nsa_layer(x2, mod3, positions, w_in, cmp_pos, cmp_w1, cmp_w2, w_out, ln_g, ln_b, seq, alpha,
              tq=256, tk=256, tq_cmp=512):
    n, d = x2.shape
    bsz = n // seq
    g, dh = NSA_KV_GROUPS, NSA_HEAD_DIM
    qw, kvw = NSA_HEADS * dh, g * dh
    cols = lambda k: w_in[:, qw + k * kvw: qw + (k + 1) * kvw]
    gl = w_in[:, qw + 6 * kvw:]
    w_main = jnp.concatenate([w_in[:, :qw], cols(2), cols(3), cols(4), cols(5)], axis=1).astype(BF16)
    w_aux = jnp.concatenate([cols(0), cols(1), jnp.pad(gl, ((0, 0), (0, GW - gl.shape[1])))],
                            axis=1).astype(BF16)
    tables = _rope_tables(positions)
    nqt = qw // GW
    kinds_main = (ROPE,) * nqt + (ROPE, PLAIN, ROPE, PLAIN)
    qkv = nsa_proj(x2, mod3, w_main, tables, kinds_main, BF16, seq, "nsa_proj_main")
    aux = nsa_proj(x2, mod3, w_aux, tables, (PLAIN, PLAIN, SIGMOID), F32, seq, "nsa_proj_aux")

    hid = cmp_w1.shape[-1]
    eye = jnp.eye(g, dtype=F32)
    w1bd = jnp.einsum('spdh,gk->spgdkh', cmp_w1.reshape(2, CMP_BLOCK, dh, hid), eye)
    w1bd = w1bd.reshape(2, CMP_BLOCK, g * dh, g * hid).astype(BF16)
    w2bd = jnp.einsum('shd,gk->sghkd', cmp_w2, eye).reshape(2, g * hid, g * dh).astype(BF16)
    pos_t = jnp.tile(cmp_pos, (1, 1, g))
    nh = seq // CMP_STRIDE
    cmp_positions = jnp.pad(positions[:, CMP_BLOCK - 1::CMP_STRIDE], ((0, 0), (0, 1)))[:, :nh]
    ctabs = tuple(t.reshape(bsz, nh, LANES) for t in _rope_tables(cmp_positions))
    kvcmp = nsa_compress(aux, w1bd, w2bd, pos_t[:, :CMP_STRIDE], pos_t[:, CMP_STRIDE:], ctabs, bsz, seq)

    c2s_t, e3 = _nsa_constants(seq, tk)
    oc, sel_t = nsa_cmp_sel(qkv, kvcmp, aux, c2s_t, bsz, seq, tq_cmp)
    o = nsa_attn(qkv, aux, sel_t, oc, e3, bsz, seq, tq, tk)
    return out_proj_ln(o, x2, mod3, 2, w_out.astype(BF16), ln_g, ln_b, seq, alpha)


MOE_ROWS = 1024
RINFO = 8
DMA_UNROLL = 8


def _router_kernel(x_ref, sh_ref, sc_ref, r_ref, tril_ref, info_ref, cnt_ref, carry_ref):
    tm = x_ref.shape[0]
    i = pl.program_id(0)

    @pl.when(i == 0)
    def _():
        carry_ref[...] = jnp.zeros_like(carry_ref)

    u = x_ref[...] * (1.0 + sc_ref[0]) + sh_ref[0]
    logits = jnp.dot(u, r_ref[...], precision=HIGHEST, preferred_element_type=F32)
    lane = lax.broadcasted_iota(jnp.int32, (1, LANES), 1)
    logits = jnp.where(lane < N_EXPERTS, logits, -jnp.inf)
    m1 = jnp.max(logits, axis=-1, keepdims=True)
    e1 = jnp.min(jnp.where(logits == m1, lane, LANES), axis=-1, keepdims=True)
    rest = jnp.where(lane == e1, -jnp.inf, logits)
    m2 = jnp.max(rest, axis=-1, keepdims=True)
    e2 = jnp.min(jnp.where(rest == m2, lane, LANES), axis=-1, keepdims=True)
    ex = jnp.exp(m2 - m1)
    g1 = 1.0 / (1.0 + ex)
    g2 = ex / (1.0 + ex)
    o1 = lane == e1
    o2 = lane == e2
    tot = jnp.where(o1 | o2, 1.0, 0.0)
    before = carry_ref[...] + jnp.dot(tril_ref[...], tot.astype(BF16), preferred_element_type=F32)
    rank1 = jnp.sum(jnp.where(o1, before, 0.0), axis=-1, keepdims=True)
    rank2 = jnp.sum(jnp.where(o2, before, 0.0), axis=-1, keepdims=True)
    carry_ref[...] += jnp.sum(tot, axis=0, keepdims=True)
    cnt_ref[...] = carry_ref[...]
    li = lax.broadcasted_iota(jnp.int32, (1, RINFO), 1)
    info = jnp.zeros((tm, RINFO), F32)
    for k, v in enumerate((e1.astype(F32), e2.astype(F32), g1, g2, rank1, rank2)):
        info = jnp.where(li == k, v, info)
    info_ref[...] = info


def moe_router(x2, mod3, router, seq, tm=512):
    n, d = x2.shape
    tpb = seq // tm
    r_pad = jnp.pad(router, ((0, 0), (0, LANES - router.shape[1])))
    tril = (jnp.arange(tm)[:, None] > jnp.arange(tm)[None, :]).astype(BF16)
    return pl.pallas_call(
        _router_kernel,
        grid=(n // tm,),
        in_specs=[
            pl.BlockSpec((tm, d), lambda i: (i, 0)),
            _mod_spec(d, tpb, 3, 1), _mod_spec(d, tpb, 4, 1),
            pl.BlockSpec((d, LANES), lambda i: (0, 0)),
            pl.BlockSpec((tm, tm), lambda i: (0, 0)),
        ],
        out_specs=[
            pl.BlockSpec((tm, RINFO), lambda i: (i, 0)),
            pl.BlockSpec((1, LANES), lambda i: (0, 0)),
        ],
        out_shape=[
            jax.ShapeDtypeStruct((n, RINFO), F32),
            jax.ShapeDtypeStruct((1, LANES), F32),
        ],
        scratch_shapes=[pltpu.VMEM((1, LANES), F32)],
        compiler_params=_cparams("arbitrary"),
        name="moe_router",
    )(x2, mod3, mod3, r_pad, tril)


def _dispatch_kernel(dest_ref, x_ref, sh_ref, sc_ref, xs_in_ref, xs_ref, u_ref, sem):
    del xs_in_ref
    tm = x_ref.shape[0]
    u_ref[...] = x_ref[...] * (1.0 + sc_ref[0]) + sh_ref[0]

    def row_copy(r, k):
        return pltpu.make_async_copy(u_ref.at[pl.ds(r, 1)], xs_ref.at[pl.ds(dest_ref[TOP_K * r + k], 1)], sem)

    def issue(r, carry):
        for k in range(TOP_K):
            row_copy(r, k).start()
        return carry

    def drain(r, carry):
        for k in range(TOP_K):
            row_copy(r, k).wait()
        return carry

    lax.fori_loop(0, tm, issue, 0, unroll=DMA_UNROLL)
    lax.fori_loop(0, tm, drain, 0, unroll=DMA_UNROLL)


def moe_dispatch(x2, mod3, dest, cap, seq, tm=256):
    n, d = x2.shape
    tpb = seq // tm
    xs0 = jnp.zeros((cap, d), F32)
    return pl.pallas_call(
        _dispatch_kernel,
        grid=(n // tm,),
        in_specs=[
            pl.BlockSpec((TOP_K * tm,), lambda i: (i,), memory_space=pltpu.SMEM),
            pl.BlockSpec((tm, d), lambda i: (i, 0)),
            _mod_spec(d, tpb, 3, 1), _mod_spec(d, tpb, 4, 1),
            pl.BlockSpec(memory_space=pl.ANY),
        ],
        out_specs=pl.BlockSpec(memory_space=pl.ANY),
        out_shape=jax.ShapeDtypeStruct((cap, d), F32),
        scratch_shapes=[pltpu.VMEM((tm, d), F32), pltpu.SemaphoreType.DMA],
        input_output_aliases={4: 0},
        compiler_params=_cparams("arbitrary"),
        name="moe_dispatch",
    )(dest, x2, mod3, mod3, xs0)


def _moe_ffn_kernel(bexp_ref, nused_ref, xs_ref, w1_ref, w3_ref, w2_ref, o_ref, xb_ref):
    del bexp_ref
    f = pl.program_id(1)
    used = pl.program_id(0) < nused_ref[0]

    @pl.when(f == 0)
    def _():
        o_ref[...] = jnp.zeros(o_ref.shape, o_ref.dtype)

    @pl.when(used & (f == 0))
    def _():
        xb_ref[...] = xs_ref[...].astype(BF16)

    @pl.when(used)
    def _():
        _swiglu_accumulate(xb_ref, w1_ref[0, 0].astype(BF16), w3_ref[0, 0].astype(BF16),
                           w2_ref[0, 0].astype(BF16), o_ref)


def moe_ffn(xs, blk_exp, n_used, w1, w3, w2, layer, tf=512):
    cap, d = xs.shape
    ff = w1.shape[3]
    grid_spec = pltpu.PrefetchScalarGridSpec(
        num_scalar_prefetch=2,
        grid=(cap // MOE_ROWS, ff // tf),
        in_specs=[
            pl.BlockSpec((MOE_ROWS, d), lambda i, f, be, nu: (i, 0)),
            pl.BlockSpec((1, 1, d, tf), lambda i, f, be, nu: (layer, be[i], 0, f)),
            pl.BlockSpec((1, 1, d, tf), lambda i, f, be, nu: (layer, be[i], 0, f)),
            pl.BlockSpec((1, 1, tf, d), lambda i, f, be, nu: (layer, be[i], f, 0)),
        ],
        out_specs=pl.BlockSpec((MOE_ROWS, d), lambda i, f, be, nu: (i, 0)),
        scratch_shapes=[pltpu.VMEM((MOE_ROWS, d), BF16)],
    )
    return pl.pallas_call(
        _moe_ffn_kernel,
        grid_spec=grid_spec,
        out_shape=jax.ShapeDtypeStruct((cap, d), F32),
        compiler_params=_cparams("parallel", "arbitrary"),
        name="moe_ffn",
    )(blk_exp, n_used, xs, w1, w3, w2)


def _combine_kernel(alpha, dest_ref, info_ref, x_ref, gf_ref, ys_ref, lg_ref, lb_ref, o_ref, buf_ref, sem):
    tm = x_ref.shape[0]

    def row_copy(r, k):
        return pltpu.make_async_copy(ys_ref.at[pl.ds(dest_ref[TOP_K * r + k], 1)],
                                     buf_ref.at[k, pl.ds(r, 1)], sem)

    def issue(r, carry):
        for k in range(TOP_K):
            row_copy(r, k).start()
        return carry

    def drain(r, carry):
        for k in range(TOP_K):
            row_copy(r, k).wait()
        return carry

    lax.fori_loop(0, tm, issue, 0, unroll=DMA_UNROLL)
    lax.fori_loop(0, tm, drain, 0, unroll=DMA_UNROLL)
    info = info_ref[...]
    y = info[:, 2:3] * buf_ref[0] + info[:, 3:4] * buf_ref[1]
    o_ref[...] = _layer_norm(alpha * x_ref[...] + (1.0 + gf_ref[0]) * y, lg_ref[...], lb_ref[...])


def moe_combine(ys, dest, info, x2, mod3, ln_g, ln_b, seq, alpha, tm=256):
    n, d = x2.shape
    tpb = seq // tm
    return pl.pallas_call(
        functools.partial(_combine_kernel, alpha),
        grid=(n // tm,),
        in_specs=[
            pl.BlockSpec((TOP_K * tm,), lambda i: (i,), memory_space=pltpu.SMEM),
            pl.BlockSpec((tm, RINFO), lambda i: (i, 0)),
            pl.BlockSpec((tm, d), lambda i: (i, 0)),
            _mod_spec(d, tpb, 5, 1),
            pl.BlockSpec(memory_space=pl.ANY),
            pl.BlockSpec((1, d), lambda i: (0, 0)),
            pl.BlockSpec((1, d), lambda i: (0, 0)),
        ],
        out_specs=pl.BlockSpec((tm, d), lambda i: (i, 0)),
        out_shape=jax.ShapeDtypeStruct((n, d), F32),
        scratch_shapes=[pltpu.VMEM((TOP_K, tm, d), F32), pltpu.SemaphoreType.DMA],
        compiler_params=_cparams("arbitrary"),
        name="moe_combine",
    )(dest, info, x2, mod3, ys, _row(ln_g), _row(ln_b))


def moe_layer(x2, mod3, router, w1, w3, w2, layer, ln_g, ln_b, seq, alpha):
    n, d = x2.shape
    info, counts = moe_router(x2, mod3, router, seq)
    sizes = counts[0, :N_EXPERTS].astype(jnp.int32)
    padded = (sizes + MOE_ROWS - 1) // MOE_ROWS * MOE_ROWS
    pend = jnp.cumsum(padded)
    pstart = pend - padded
    cap = n * TOP_K + N_EXPERTS * MOE_ROWS
    n_blk = cap // MOE_ROWS
    blk_lo = jnp.arange(n_blk, dtype=jnp.int32) * MOE_ROWS
    blk_exp = jnp.minimum(jnp.sum(blk_lo[:, None] >= pend[None, :], axis=1), N_EXPERTS - 1).astype(jnp.int32)
    n_used = (pend[-1:] // MOE_ROWS).astype(jnp.int32)
    ids = jnp.arange(N_EXPERTS, dtype=jnp.int32)
    e = info[:, 0:TOP_K].astype(jnp.int32)
    start = jnp.sum(jnp.where(e[:, :, None] == ids, pstart, 0), axis=-1)
    dest = (start + info[:, 4:4 + TOP_K].astype(jnp.int32)).reshape(-1)
    xs = moe_dispatch(x2, mod3, dest, cap, seq)
    ys = moe_ffn(xs, blk_exp, n_used, w1, w3, w2, layer)
    return moe_combine(ys, dest, info, x2, mod3, ln_g, ln_b, seq, alpha)


def kernel(x, c, positions, ada_w, ada_b, ln_g, ln_b, conv_w_in, conv_b_in, conv_dw, conv_dw_b, conv_ln_g, conv_ln_b, conv_w_out, conv_b_out, nsa_w_in, nsa_cmp_pos, nsa_cmp_w1, nsa_cmp_w2, nsa_w_out, pool_w, pool_scale, ffn_w1, ffn_w3, ffn_w2, moe_router, moe_w1, moe_w3, moe_w2):
    bsz, seq, d = x.shape
    depth = ada_w.shape[0]
    alpha = (2 * depth) ** 0.25
    mod = ada_mod(c, ada_w, ada_b)
    x2 = x.reshape(bsz * seq, d)
    mods = mod.reshape(depth, bsz, 1, 6 * d)
    for i in range(depth):
        mod3 = mods[i]
        j = i // N_MIXERS
        if i % N_MIXERS == 0:
            h = conv_in(x2, mod3, conv_w_in[j].astype(BF16), conv_b_in[j], seq)
            x2 = conv_tail(h, x2, mod3, conv_dw[j], conv_dw_b[j], conv_ln_g[j], conv_ln_b[j],
                           conv_w_out[j].astype(BF16), conv_b_out[j], ln_g[i, 0], ln_b[i, 0], seq, alpha)
        elif i % N_MIXERS == 1:
            x2 = nsa_layer(x2, mod3, positions, nsa_w_in[j], nsa_cmp_pos[j], nsa_cmp_w1[j], nsa_cmp_w2[j],
                           nsa_w_out[j], ln_g[i, 0], ln_b[i, 0], seq, alpha)
        else:
            x2 = pool_layer(x2, mod3, pool_w[j].astype(BF16), pool_scale[j], ln_g[i, 0], ln_b[i, 0],
                            seq, alpha)
        j = i // 2
        if i % 2 == 0:
            x2 = ffn_dense(x2, mod3, ffn_w1, ffn_w3, ffn_w2, j, ln_g[i, 1], ln_b[i, 1], seq, alpha)
        else:
            x2 = moe_layer(x2, mod3, moe_router[j], moe_w1, moe_w3, moe_w2, j,
                           ln_g[i, 1], ln_b[i, 1], seq, alpha)
    return x2.reshape(bsz, seq, d)
```

```python
import functools

import jax
import jax.numpy as jnp
from jax import lax
from jax.experimental import pallas as pl
from jax.experimental.pallas import tpu as pltpu

F32 = jnp.float32
BF16 = jnp.bfloat16
HIGHEST = lax.Precision.HIGHEST

N_MIXERS = 3
CONV_WIDTH = 31
NSA_HEADS = 16
NSA_KV_GROUPS = 4
NSA_HEADS_PER_GROUP = NSA_HEADS // NSA_KV_GROUPS
NSA_HEAD_DIM = 64
CMP_BLOCK = 32
CMP_STRIDE = 16
SLC_BLOCK = 64
SLC_TOP_N = 16
WINDOW = 512
ROPE_THETA = 500000.0
ROT_DIM = NSA_HEAD_DIM // 4
POOL_WINDOWS = (2, 4, 8, 16)
N_EXPERTS = 8
TOP_K = 2
LN_EPS = 1e-5
NEG_INF = -1e30
SEL_BIG = 1e9

LANES = 128
SUBLANES = 8
VMEM_LIMIT = 56 * 1024 * 1024


def _cparams(*sem):
    return pltpu.CompilerParams(dimension_semantics=sem, vmem_limit_bytes=VMEM_LIMIT)


def _layer_norm(v, g, b):
    mu = jnp.mean(v, axis=-1, keepdims=True)
    d = v - mu
    var = jnp.mean(d * d, axis=-1, keepdims=True)
    return d * lax.rsqrt(var + LN_EPS) * g + b


def _silu(v):
    return v * jax.nn.sigmoid(v)


def _row(v):
    return v.reshape(1, -1)


def _ada_kernel(c_ref, w_ref, b_ref, o_ref):
    cond = _silu(c_ref[...])
    o_ref[0] = jnp.dot(cond, w_ref[0], precision=HIGHEST, preferred_element_type=F32) + b_ref[0]


def ada_mod(c, ada_w, ada_b):
    depth, d, d6 = ada_w.shape
    b = c.shape[0]
    tn = d6 // 4
    return pl.pallas_call(
        _ada_kernel,
        grid=(depth, d6 // tn),
        in_specs=[
            pl.BlockSpec((b, d), lambda i, j: (0, 0)),
            pl.BlockSpec((1, d, tn), lambda i, j: (i, 0, j)),
            pl.BlockSpec((1, 1, tn), lambda i, j: (i, 0, j)),
        ],
        out_specs=pl.BlockSpec((1, b, tn), lambda i, j: (i, 0, j)),
        out_shape=jax.ShapeDtypeStruct((depth, b, d6), F32),
        compiler_params=_cparams("parallel", "parallel"),
        name="ada_mod",
    )(c, ada_w, ada_b.reshape(depth, 1, d6))


def _mod_spec(d, tiles_per_batch, chunk, ngrid):
    if ngrid == 1:
        return pl.BlockSpec((1, 1, d), lambda i: (i // tiles_per_batch, 0, chunk))
    return pl.BlockSpec((1, 1, d), lambda i, j: (i // tiles_per_batch, 0, chunk))


CONV_IN_COLS = 512


def _conv_in_kernel(x_ref, sh_ref, sc_ref, w_ref, b_ref, h_ref):
    d = h_ref.shape[1]
    u = (x_ref[...] * (1.0 + sc_ref[0]) + sh_ref[0]).astype(BF16)
    for c0 in range(0, d, CONV_IN_COLS):
        a = jnp.dot(u, w_ref[:, c0:c0 + CONV_IN_COLS], preferred_element_type=F32) + b_ref[:, c0:c0 + CONV_IN_COLS]
        g = jnp.dot(u, w_ref[:, d + c0:d + c0 + CONV_IN_COLS], preferred_element_type=F32) + b_ref[
            :, d + c0:d + c0 + CONV_IN_COLS]
        h_ref[:, c0:c0 + CONV_IN_COLS] = a * jax.nn.sigmoid(g)


def conv_in(x2, mod3, w_in_bf, b_in, seq, tm=512):
    n, d = x2.shape
    tpb = seq // tm
    return pl.pallas_call(
        _conv_in_kernel,
        grid=(n // tm,),
        in_specs=[
            pl.BlockSpec((tm, d), lambda i: (i, 0)),
            _mod_spec(d, tpb, 0, 1),
            _mod_spec(d, tpb, 1, 1),
            pl.BlockSpec((d, 2 * d), lambda i: (0, 0)),
            pl.BlockSpec((1, 2 * d), lambda i: (0, 0)),
        ],
        out_specs=pl.BlockSpec((tm, d), lambda i: (i, 0)),
        out_shape=jax.ShapeDtypeStruct((n, d), F32),
        compiler_params=_cparams("parallel"),
        name="conv_in",
    )(x2, mod3, mod3, w_in_bf, _row(b_in))


CONV_HALO = 32
CONV_ROWS = 16


def _conv_tail_kernel(tpb, alpha, hc_ref, hp_ref, x_ref, gm_ref, dw_ref, dwb_ref, cg_ref, cb_ref,
                      wo_ref, bo_ref, lg_ref, lb_ref, o_ref, win_ref, cv_ref):
    tm = hc_ref.shape[0]
    first = (pl.program_id(0) % tpb) == 0
    win_ref[0, 0:CONV_HALO, :] = jnp.where(first, 0.0, hp_ref[...])
    win_ref[0, CONV_HALO:CONV_HALO + tm, :] = hc_ref[...]
    keep = tm + CONV_HALO - SUBLANES
    for s in range(1, SUBLANES):
        win_ref[s, 0:keep, :] = win_ref[0, s:s + keep, :]
    off = CONV_HALO - (CONV_WIDTH - 1)
    halves = CONV_ROWS // SUBLANES
    for rc in range(tm // CONV_ROWS):
        r0 = rc * CONV_ROWS
        accs = [None] * halves
        for k in range(CONV_WIDTH):
            j = off + k
            a0 = r0 + j - j % SUBLANES
            tap = dw_ref[k]
            for i in range(halves):
                term = win_ref[j % SUBLANES, a0 + i * SUBLANES:a0 + (i + 1) * SUBLANES, :] * tap
                accs[i] = term + dwb_ref[...] if accs[i] is None else accs[i] + term
        for i in range(halves):
            cv_ref[r0 + i * SUBLANES:r0 + (i + 1) * SUBLANES, :] = accs[i]
    hn = _silu(_layer_norm(cv_ref[...], cg_ref[...], cb_ref[...]))
    y = jnp.dot(hn.astype(BF16), wo_ref[...], preferred_element_type=F32) + bo_ref[...]
    o_ref[...] = _layer_norm(alpha * x_ref[...] + (1.0 + gm_ref[0]) * y, lg_ref[...], lb_ref[...])


def conv_tail(h, x2, mod3, dw, dw_b, cln_g, cln_b, w_out_bf, b_out, ln_g, ln_b, seq, alpha, tm=256):
    n, d = x2.shape
    tpb = seq // tm
    hb = tm // CONV_HALO
    full = lambda shape: pl.BlockSpec(shape, lambda i: (0, 0))
    return pl.pallas_call(
        functools.partial(_conv_tail_kernel, tpb, alpha),
        grid=(n // tm,),
        in_specs=[
            pl.BlockSpec((tm, d), lambda i: (i, 0)),
            pl.BlockSpec((CONV_HALO, d), lambda i: (jnp.maximum(i * hb - 1, 0), 0)),
            pl.BlockSpec((tm, d), lambda i: (i, 0)),
            _mod_spec(d, tpb, 2, 1),
            pl.BlockSpec((CONV_WIDTH, SUBLANES, d), lambda i: (0, 0, 0)), full((1, d)), full((1, d)), full((1, d)),
            full((d, d)), full((1, d)), full((1, d)), full((1, d)),
        ],
        out_specs=pl.BlockSpec((tm, d), lambda i: (i, 0)),
        out_shape=jax.ShapeDtypeStruct((n, d), F32),
        scratch_shapes=[pltpu.VMEM((SUBLANES, tm + CONV_HALO, d), F32), pltpu.VMEM((tm, d), F32)],
        compiler_params=_cparams("parallel"),
        name="conv_tail",
    )(h, h, x2, mod3, jnp.broadcast_to(dw[:, None, :], (CONV_WIDTH, SUBLANES, d)), _row(dw_b), _row(cln_g),
      _row(cln_b), w_out_bf, _row(b_out), _row(ln_g), _row(ln_b))


FFN_ROWS = 512


def _swiglu_accumulate(u_ref, w1, w3, w2, acc_ref):
    for r0 in range(0, u_ref.shape[0], FFN_ROWS):
        u = u_ref[r0:r0 + FFN_ROWS, :]
        h = _silu(jnp.dot(u, w1, preferred_element_type=F32)) * jnp.dot(u, w3, preferred_element_type=F32)
        acc_ref[r0:r0 + FFN_ROWS, :] += jnp.dot(h.astype(BF16), w2, preferred_element_type=F32)


def _ffn_kernel(alpha, x_ref, sh_ref, sc_ref, gf_ref, w1_ref, w3_ref, w2_ref, lg_ref, lb_ref,
                o_ref, u_ref, acc_ref):
    f = pl.program_id(1)

    @pl.when(f == 0)
    def _():
        u_ref[...] = (x_ref[...] * (1.0 + sc_ref[0]) + sh_ref[0]).astype(BF16)
        acc_ref[...] = jnp.zeros(acc_ref.shape, F32)

    _swiglu_accumulate(u_ref, w1_ref[0].astype(BF16), w3_ref[0].astype(BF16), w2_ref[0].astype(BF16), acc_ref)

    @pl.when(f == pl.num_programs(1) - 1)
    def _():
        o_ref[...] = _layer_norm(alpha * x_ref[...] + (1.0 + gf_ref[0]) * acc_ref[...],
                                 lg_ref[...], lb_ref[...])


def ffn_dense(x2, mod3, w1, w3, w2, layer, ln_g, ln_b, seq, alpha, tm=1024, tf=512):
    n, d = x2.shape
    ff = w1.shape[2]
    tpb = seq // tm
    return pl.pallas_call(
        functools.partial(_ffn_kernel, alpha),
        grid=(n // tm, ff // tf),
        in_specs=[
            pl.BlockSpec((tm, d), lambda i, f: (i, 0)),
            _mod_spec(d, tpb, 3, 2), _mod_spec(d, tpb, 4, 2), _mod_spec(d, tpb, 5, 2),
            pl.BlockSpec((1, d, tf), lambda i, f: (layer, 0, f)),
            pl.BlockSpec((1, d, tf), lambda i, f: (layer, 0, f)),
            pl.BlockSpec((1, tf, d), lambda i, f: (layer, f, 0)),
            pl.BlockSpec((1, d), lambda i, f: (0, 0)),
            pl.BlockSpec((1, d), lambda i, f: (0, 0)),
        ],
        out_specs=pl.BlockSpec((tm, d), lambda i, f: (i, 0)),
        out_shape=jax.ShapeDtypeStruct((n, d), F32),
        scratch_shapes=[pltpu.VMEM((tm, d), BF16), pltpu.VMEM((tm, d), F32)],
        compiler_params=_cparams("parallel", "arbitrary"),
        name="ffn_dense",
    )(x2, mod3, mod3, mod3, w1, w3, w2, _row(ln_g), _row(ln_b))


POOL_HALO = 16


def _pool_kernel(tpb, alpha, xc_ref, xp_ref, sh_ref, sc_ref, gm_ref, w_ref, ps_ref, lg_ref, lb_ref,
                 o_ref, win_ref):
    tm, d = xc_ref.shape
    gw = d // len(POOL_WINDOWS)
    tile = pl.program_id(0) % tpb
    first = tile == 0
    scale = 1.0 + sc_ref[0]
    up = xp_ref[...] * scale + sh_ref[0]
    uc = xc_ref[...] * scale + sh_ref[0]
    win_ref[0:POOL_HALO, :] = jnp.where(first, 0.0, up)
    win_ref[POOL_HALO:POOL_HALO + tm, :] = uc
    t = tile * tm + lax.broadcasted_iota(jnp.int32, (tm, 1), 0)
    ys = []
    for gi, w in enumerate(POOL_WINDOWS):
        c0 = gi * gw
        s = uc[:, c0:c0 + gw]
        for k in range(1, w):
            s = s + win_ref[POOL_HALO - k:POOL_HALO - k + tm, c0:c0 + gw]
        cnt = jnp.minimum(t + 1, w).astype(F32)
        p = s / cnt - uc[:, c0:c0 + gw]
        ys.append(jnp.dot(p.astype(BF16), w_ref[gi], preferred_element_type=F32))
    y = jnp.concatenate(ys, axis=-1) * ps_ref[...]
    o_ref[...] = _layer_norm(alpha * xc_ref[...] + (1.0 + gm_ref[0]) * y, lg_ref[...], lb_ref[...])


def pool_layer(x2, mod3, w_grp_bf, pool_scale, ln_g, ln_b, seq, alpha, tm=512):
    n, d = x2.shape
    tpb = seq // tm
    hb = tm // POOL_HALO
    ng, gw, _ = w_grp_bf.shape
    return pl.pallas_call(
        functools.partial(_pool_kernel, tpb, alpha),
        grid=(n // tm,),
        in_specs=[
            pl.BlockSpec((tm, d), lambda i: (i, 0)),
            pl.BlockSpec((POOL_HALO, d), lambda i: (jnp.maximum(i * hb - 1, 0), 0)),
            _mod_spec(d, tpb, 0, 1), _mod_spec(d, tpb, 1, 1), _mod_spec(d, tpb, 2, 1),
            pl.BlockSpec((ng, gw, gw), lambda i: (0, 0, 0)),
            pl.BlockSpec((1, d), lambda i: (0, 0)),
            pl.BlockSpec((1, d), lambda i: (0, 0)),
            pl.BlockSpec((1, d), lambda i: (0, 0)),
        ],
        out_specs=pl.BlockSpec((tm, d), lambda i: (i, 0)),
        out_shape=jax.ShapeDtypeStruct((n, d), F32),
        scratch_shapes=[pltpu.VMEM((tm + POOL_HALO, d), F32)],
        compiler_params=_cparams("parallel"),
        name="pool_layer",
    )(x2, x2, mod3, mod3, mod3, w_grp_bf, _row(pool_scale), _row(ln_g), _row(ln_b))


GW = NSA_HEADS_PER_GROUP * NSA_HEAD_DIM
PLAIN, ROPE, SIGMOID = 0, 1, 2


def _rope_tables(positions):
    half = ROT_DIM // 2
    inv = ROPE_THETA ** (-jnp.arange(half, dtype=F32) * 2.0 / ROT_DIM)
    ang = positions.astype(F32)[..., None] * inv
    cos, sin = jnp.cos(ang), jnp.sin(ang)
    l64 = jnp.arange(LANES) % NSA_HEAD_DIM
    idx = l64 % half
    lo, hi = l64 < half, (l64 >= half) & (l64 < ROT_DIM)
    ct = jnp.where(l64 < ROT_DIM, cos[..., idx], 1.0)
    sp = jnp.where(hi, sin[..., idx], 0.0)
    sm = jnp.where(lo, -sin[..., idx], 0.0)
    flat = lambda a: a.reshape(-1, LANES)
    return flat(ct), flat(sp), flat(sm)


def _rope(v, ct, sp, sm):
    half = ROT_DIM // 2
    outs = []
    for c0 in range(0, v.shape[1], LANES):
        vc = v[:, c0:c0 + LANES]
        outs.append(vc * ct + pltpu.roll(vc, half, 1) * sp + pltpu.roll(vc, LANES - half, 1) * sm)
    return outs[0] if len(outs) == 1 else jnp.concatenate(outs, axis=1)


def _proj_kernel(kinds, x_ref, sh_ref, sc_ref, w_ref, ct_ref, sp_ref, sm_ref, o_ref):
    u = (x_ref[...] * (1.0 + sc_ref[0]) + sh_ref[0]).astype(BF16)
    for j, kind in enumerate(kinds):
        acc = jnp.dot(u, w_ref[:, j * GW:(j + 1) * GW], preferred_element_type=F32)
        if kind == ROPE:
            acc = _rope(acc, ct_ref[...], sp_ref[...], sm_ref[...])
        elif kind == SIGMOID:
            acc = jax.nn.sigmoid(acc)
        o_ref[:, j * GW:(j + 1) * GW] = acc.astype(o_ref.dtype)


def nsa_proj(x2, mod3, w_bf, tables, kinds, out_dtype, seq, name, tm=512):
    n, d = x2.shape
    cols = w_bf.shape[1]
    tpb = seq // tm
    assert cols == GW * len(kinds)
    tab = pl.BlockSpec((tm, LANES), lambda i: (i, 0))
    return pl.pallas_call(
        functools.partial(_proj_kernel, kinds),
        grid=(n // tm,),
        in_specs=[
            pl.BlockSpec((tm, d), lambda i: (i, 0)),
            _mod_spec(d, tpb, 0, 1), _mod_spec(d, tpb, 1, 1),
            pl.BlockSpec((d, cols), lambda i: (0, 0)),
            tab, tab, tab,
        ],
        out_specs=pl.BlockSpec((tm, cols), lambda i: (i, 0)),
        out_shape=jax.ShapeDtypeStruct((n, cols), out_dtype),
        compiler_params=_cparams("parallel"),
        name=name,
    )(x2, mod3, mod3, w_bf, *tables)


def _gelu_tanh(v):
    return 0.5 * v * (1.0 + jnp.tanh(0.7978845608028654 * (v + 0.044715 * (v * v * v))))


def _compress_kernel(xl_ref, xh_ref, w1_ref, w2_ref, pa_ref, pb_ref, ct_ref, sp_ref, sm_ref, o_ref):
    nh = o_ref.shape[2]
    a = None
    bm = None
    for p in range(CMP_STRIDE):
        hp = jnp.concatenate([xl_ref[pl.ds(p, nh, stride=CMP_STRIDE), :],
                              xh_ref[pl.ds(p, nh, stride=CMP_STRIDE), :]], axis=1)
        da = jnp.dot((hp + pa_ref[0, p:p + 1, :]).astype(BF16), w1_ref[0, p], preferred_element_type=F32)
        db = jnp.dot((hp + pb_ref[0, p:p + 1, :]).astype(BF16), w1_ref[0, CMP_STRIDE + p],
                     preferred_element_type=F32)
        a = da if a is None else a + da
        bm = db if bm is None else bm + db
    hid = _gelu_tanh(a + pltpu.roll(bm, nh - 1, 0))
    out = jnp.dot(hid.astype(BF16), w2_ref[0], preferred_element_type=F32)

    @pl.when(pl.program_id(1) == 0)
    def _():
        o_ref[0, 0] = _rope(out, ct_ref[0], sp_ref[0], sm_ref[0]).astype(o_ref.dtype)

    @pl.when(pl.program_id(1) != 0)
    def _():
        o_ref[0, 0] = out.astype(o_ref.dtype)


def nsa_compress(aux, w1bd, w2bd, pos_a, pos_b, ctabs, bsz, seq):
    nh = seq // CMP_STRIDE
    hid = w1bd.shape[-1]
    ctab = pl.BlockSpec((1, nh, LANES), lambda b, s: (b, 0, 0))
    return pl.pallas_call(
        _compress_kernel,
        grid=(bsz, 2),
        in_specs=[
            pl.BlockSpec((seq, LANES), lambda b, s: (b, 2 * s)),
            pl.BlockSpec((seq, LANES), lambda b, s: (b, 2 * s + 1)),
            pl.BlockSpec((1, CMP_BLOCK, GW, hid), lambda b, s: (s, 0, 0, 0)),
            pl.BlockSpec((1, hid, GW), lambda b, s: (s, 0, 0)),
            pl.BlockSpec((1, CMP_STRIDE, GW), lambda b, s: (s, 0, 0)),
            pl.BlockSpec((1, CMP_STRIDE, GW), lambda b, s: (s, 0, 0)),
            ctab, ctab, ctab,
        ],
        out_specs=pl.BlockSpec((1, 1, nh, GW), lambda b, s: (s, b, 0, 0)),
        out_shape=jax.ShapeDtypeStruct((2, bsz, nh, GW), BF16),
        compiler_params=_cparams("parallel", "arbitrary"),
        name="nsa_compress",
    )(aux, aux, w1bd, w2bd, pos_a, pos_b, *ctabs)


N_BRANCH = 3


def _gate_rows(gt_ref, branch, tq):
    head0 = pl.program_id(1) * NSA_HEADS_PER_GROUP
    return jnp.concatenate([gt_ref[pl.ds((head0 + r) * N_BRANCH + branch, 1), :]
                            for r in range(NSA_HEADS_PER_GROUP)], axis=1)


def _cmp_sel_kernel(q_ref, kc_ref, vct_ref, g_ref, c2s_ref, oc_ref, sel_ref, qz_ref, gt_ref):
    tq = q_ref.shape[0]
    nc = kc_ref.shape[2]
    r4 = NSA_HEADS_PER_GROUP
    dh = NSA_HEAD_DIM
    g0 = pl.multiple_of(pl.program_id(1) * dh, dh)
    q0 = pl.program_id(2) * tq
    scale = dh ** -0.5
    q_t = (q_ref[...].astype(F32) * scale).T.astype(BF16)
    qz_ref[...] = jnp.zeros(qz_ref.shape, BF16)
    for h in range(r4):
        qz_ref[pl.ds(g0, dh), h * tq:(h + 1) * tq] = q_t[h * dh:(h + 1) * dh, :]
    kc = kc_ref[0, 0]
    vct = vct_ref[0, pl.ds(g0, dh), :]
    t_lanes = q0 + lax.broadcasted_iota(jnp.int32, (1, tq), 1)
    cend = lax.broadcasted_iota(jnp.int32, (nc, 1), 0) * CMP_STRIDE + (CMP_BLOCK - 1)
    ok = cend <= t_lanes
    scores = [jnp.dot(kc, qz_ref[:, h * tq:(h + 1) * tq], preferred_element_type=F32) for h in range(r4)]
    probs = []
    psum = None
    for st in scores:
        st = jnp.where(ok, st, NEG_INF)
        et = jnp.exp(st - jnp.max(st, axis=0, keepdims=True))
        pt = jnp.where(ok, et * (1.0 / jnp.sum(et, axis=0, keepdims=True)), 0.0)
        probs.append(pt.astype(BF16))
        psum = pt if psum is None else psum + pt
    oc_t = [jnp.dot(vct, pt, preferred_element_type=F32) for pt in probs]
    gt_ref[...] = g_ref[:, 0:LANES].T
    gate = _gate_rows(gt_ref, 0, tq)
    oc_ref[...] = jnp.concatenate([oc_t[h] * gate[:, h * tq:(h + 1) * tq] for h in range(r4)], axis=0).T

    imp = jnp.dot(c2s_ref[...], psum, precision=HIGHEST, preferred_element_type=F32)
    ns = imp.shape[0]
    jblk = lax.broadcasted_iota(jnp.int32, (ns, 1), 0)
    cur = t_lanes // SLC_BLOCK
    forced = (jblk == 0) | (jblk == cur) | (jblk == cur - 1)
    valid = jblk <= cur
    score = jnp.where(forced, SEL_BIG, jnp.where(valid, imp, -SEL_BIG))
    nb = ns // SUBLANES
    blocks = [score[b * SUBLANES:(b + 1) * SUBLANES, :] for b in range(nb)]
    jrow = lax.broadcasted_iota(jnp.int32, (SUBLANES, 1), 0)
    cnt = [jnp.zeros((SUBLANES, tq), F32) for _ in range(nb)]
    for i in range(ns):
        bi = i // SUBLANES
        si = score[i:i + 1, :]
        for b in range(nb):
            if b < bi:
                ahead = si > blocks[b]
            elif b > bi:
                ahead = si >= blocks[b]
            else:
                ahead = (si > blocks[b]) | ((si == blocks[b]) & (jrow > i - bi * SUBLANES))
            cnt[b] = cnt[b] + jnp.where(ahead, 1.0, 0.0)
    top_n = min(SLC_TOP_N, ns)
    sel_ref[0, 0] = jnp.where((jnp.concatenate(cnt, axis=0) < top_n) & valid, 1.0, 0.0).astype(sel_ref.dtype)


def nsa_cmp_sel(qkv, kvcmp, aux, c2s_t, bsz, seq, tq=256):
    n = qkv.shape[0]
    g = NSA_KV_GROUPS
    nq = seq // tq
    nc = kvcmp.shape[2]
    ns = seq // SLC_BLOCK
    row = lambda b, gi, i: b * nq + i
    vc_t = jnp.swapaxes(kvcmp[1], 1, 2)
    return pl.pallas_call(
        _cmp_sel_kernel,
        grid=(bsz, g, nq),
        in_specs=[
            pl.BlockSpec((tq, GW), lambda b, gi, i: (row(b, gi, i), gi)),
            pl.BlockSpec((1, 1, nc, GW), lambda b, gi, i: (0, b, 0, 0)),
            pl.BlockSpec((1, GW, nc), lambda b, gi, i: (b, 0, 0)),
            pl.BlockSpec((tq, GW), lambda b, gi, i: (row(b, gi, i), 2)),
            pl.BlockSpec((ns, nc), lambda b, gi, i: (0, 0)),
        ],
        out_specs=[
            pl.BlockSpec((tq, GW), lambda b, gi, i: (row(b, gi, i), gi)),
            pl.BlockSpec((1, 1, ns, tq), lambda b, gi, i: (b, gi, 0, i)),
        ],
        out_shape=[
            jax.ShapeDtypeStruct((n, g * GW), F32),
            jax.ShapeDtypeStruct((bsz, g, ns, seq), BF16),
        ],
        scratch_shapes=[pltpu.VMEM((GW, NSA_HEADS_PER_GROUP * tq), BF16), pltpu.VMEM((LANES, tq), F32)],
        compiler_params=_cparams("parallel", "parallel", "parallel"),
        name="nsa_cmp_sel",
    )(qkv, kvcmp, vc_t, aux, c2s_t)


M_FLOOR = -1e29


ATTN_COLS = 128


def _softmax_tile_t(s_ref, b_ref, p_ref, m, l):
    tq = b_ref.shape[1]
    m_out, l_out, a_out = [], [], []
    for c0 in range(0, s_ref.shape[1], ATTN_COLS):
        b0 = c0 % tq
        sc = s_ref[:, c0:c0 + ATTN_COLS] + b_ref[:, b0:b0 + ATTN_COLS]
        m_old = m[:, c0:c0 + ATTN_COLS]
        m_new = jnp.maximum(m_old, jnp.max(sc, axis=0, keepdims=True))
        a = jnp.exp2(m_old - m_new)
        p = jnp.exp2(sc - m_new)
        l_out.append(a * l[:, c0:c0 + ATTN_COLS] + jnp.sum(p, axis=0, keepdims=True))
        m_out.append(m_new)
        a_out.append(a)
        p_ref[:, c0:c0 + ATTN_COLS] = p.astype(BF16)
    cat = lambda parts: jnp.concatenate(parts, axis=1)
    return cat(m_out), cat(l_out), cat(a_out)


def _flash_branch_t(kt_lo, n_kt, scores, values, bufs):
    dh = NSA_HEAD_DIM
    (s_x, b_x, p_x), (s_y, b_y, p_y) = bufs
    width = s_x.shape[1]
    last = n_kt - 1

    def half_step(kt, cur, nxt, state):
        a_prev, m, l, acc = state
        scores(kt + 1, nxt[0], nxt[1])
        acc = a_prev * acc + jnp.dot(values(jnp.clip(kt - 1, kt_lo, last)), nxt[2][...],
                                     preferred_element_type=F32)
        m, l, a_cur = _softmax_tile_t(cur[0], cur[1], cur[2], m, l)
        return a_cur, m, l, acc

    def body(j, state):
        kt = kt_lo + 2 * j
        state = half_step(kt, (s_x, b_x, p_x), (s_y, b_y, p_y), state)
        return half_step(kt + 1, (s_y, b_y, p_y), (s_x, b_x, p_x), state)

    scores(kt_lo, s_x, b_x)
    p_y[...] = jnp.zeros(p_y.shape, BF16)
    init = (jnp.ones((1, width), F32), jnp.full((1, width), M_FLOOR, F32), jnp.zeros((1, width), F32),
            jnp.zeros((dh, width), F32))
    trips = (n_kt - kt_lo + 1) // 2
    a_last, _, l, acc = lax.fori_loop(0, trips, body, init)
    kt_end = kt_lo + 2 * trips - 1
    acc = a_last * acc + jnp.dot(values(jnp.minimum(kt_end, last)), p_y[...], preferred_element_type=F32)
    return acc / l


def _flash_fixed_t(n_tiles, scores, values, bufs):
    dh = NSA_HEAD_DIM
    width = bufs[0][0].shape[1]
    scores(0, bufs[0][0], bufs[0][1])
    a_prev = m = l = acc = None
    for j in range(n_tiles):
        cur, nxt = bufs[j % 2], bufs[(j + 1) % 2]
        if j + 1 < n_tiles:
            scores(j + 1, nxt[0], nxt[1])
        if j > 0:
            pv = jnp.dot(values(j - 1), nxt[2][...], preferred_element_type=F32)
            acc = pv if acc is None else a_prev * acc + pv
        if m is None:
            m, l = jnp.full((1, width), M_FLOOR, F32), jnp.zeros((1, width), F32)
        m, l, a_prev = _softmax_tile_t(cur[0], cur[1], cur[2], m, l)
    pv = jnp.dot(values(n_tiles - 1), bufs[(n_tiles - 1) % 2][2][...], preferred_element_type=F32)
    acc = pv if acc is None else a_prev * acc + pv
    return acc / l


LOG2E = 1.4426950408889634
MASK_NONE, MASK_LOWER, MASK_UPPER, MASK_ALL = 0, 1, 2, 3


def _window_masks(tk):
    return (MASK_UPPER,) + (MASK_NONE,) * (WINDOW // tk - 1) + (MASK_LOWER,)


def _attn_kernel(tk, q_ref, ks_ref, vst_ref, kw_ref, vwt_ref, sel_ref, e_ref, cm_ref, g_ref,
                 oc_ref, o_ref, qz_ref, gt_ref, sx_ref, bx_ref, px_ref, sy_ref, by_ref, py_ref):
    tq = q_ref.shape[0]
    r4 = NSA_HEADS_PER_GROUP
    dh = NSA_HEAD_DIM
    g0 = pl.multiple_of(pl.program_id(1) * dh, dh)
    q0 = pl.program_id(2) * tq
    scale = dh ** -0.5 * LOG2E
    q_t = (q_ref[...].astype(F32) * scale).T.astype(BF16)
    qz_ref[...] = jnp.zeros(qz_ref.shape, BF16)
    for h in range(r4):
        qz_ref[pl.ds(g0, dh), h * tq:(h + 1) * tq] = q_t[h * dh:(h + 1) * dh, :]
    sel_bias = ((sel_ref[0, 0].astype(F32) - 1.0) * -NEG_INF).astype(BF16)
    n_kt = pl.program_id(2) + 1

    def tile_scores(k_ref, kt, s_out):
        k0 = pl.multiple_of(jnp.minimum(kt, n_kt - 1) * tk, tk)
        s_out[...] = jnp.dot(k_ref[pl.ds(k0, tk), :], qz_ref[...], preferred_element_type=F32)

    def sel_scores(kt, s_out, b_out):
        tile_scores(ks_ref, kt, s_out)
        which = jnp.where(kt >= n_kt, MASK_ALL, jnp.where(kt == n_kt - 1, MASK_LOWER, MASK_NONE))
        b_out[...] = jnp.dot(e_ref[jnp.minimum(kt, n_kt - 1)], sel_bias,
                             preferred_element_type=F32) + cm_ref[which]

    win_masks = _window_masks(tk)
    n_win = len(win_masks)
    win_kt = lambda j: n_kt - n_win + j

    def win_scores(j, s_out, b_out):
        kt = win_kt(j)
        tile_scores(kw_ref, jnp.maximum(kt, 0), s_out)
        b_out[...] = cm_ref[jnp.where(kt >= 0, win_masks[j], MASK_ALL)]

    bufs = ((sx_ref, bx_ref, px_ref), (sy_ref, by_ref, py_ref))
    o_s = _flash_branch_t(0, n_kt, sel_scores, lambda kt: vst_ref[0, kt, pl.ds(g0, dh), :], bufs)
    o_w = _flash_fixed_t(n_win, win_scores,
                         lambda j: vwt_ref[0, jnp.maximum(win_kt(j), 0), pl.ds(g0, dh), :], bufs)

    gt_ref[...] = g_ref[:, 0:LANES].T
    o_t = _gate_rows(gt_ref, 1, tq) * o_s + _gate_rows(gt_ref, 2, tq) * o_w
    o = jnp.concatenate([o_t[:, h * tq:(h + 1) * tq] for h in range(r4)], axis=0).T
    o_ref[...] = (oc_ref[...] + o).astype(o_ref.dtype)


def nsa_attn(qkv, aux, sel_t, oc, e3t, bsz, seq, tq=256, tk=256):
    n = qkv.shape[0]
    g = NSA_KV_GROUPS
    r4 = NSA_HEADS_PER_GROUP
    nq = seq // tq
    nkt = seq // tk
    ns = seq // SLC_BLOCK
    nqt = NSA_HEADS * NSA_HEAD_DIM // GW
    assert tq == tk and WINDOW % tk == 0, "the constant mask table assumes square tiles dividing the window"
    row = lambda b, gi, i: b * nq + i
    kv = lambda c: pl.BlockSpec((seq, GW), lambda b, gi, i: (b, nqt + c))
    lower = jnp.arange(tk)[:, None] <= jnp.arange(tq)[None, :]
    keep = jnp.stack([jnp.ones_like(lower), lower, ~lower, jnp.zeros_like(lower)])
    cmask = jnp.where(keep, 0.0, NEG_INF).astype(F32)
    vt = lambda c: qkv[:, (nqt + c) * GW:(nqt + c + 1) * GW].reshape(bsz, nkt, tk, GW).transpose(0, 1, 3, 2)
    vt_spec = pl.BlockSpec((1, nkt, GW, tk), lambda b, gi, i: (b, 0, 0, 0))
    return pl.pallas_call(
        functools.partial(_attn_kernel, tk),
        grid=(bsz, g, nq),
        in_specs=[
            pl.BlockSpec((tq, GW), lambda b, gi, i: (row(b, gi, i), gi)),
            kv(0), vt_spec, kv(2), vt_spec,
            pl.BlockSpec((1, 1, ns, tq), lambda b, gi, i: (b, gi, 0, i)),
            pl.BlockSpec((nkt, tk, ns), lambda b, gi, i: (0, 0, 0)),
            pl.BlockSpec((4, tk, tq), lambda b, gi, i: (0, 0, 0)),
            pl.BlockSpec((tq, GW), lambda b, gi, i: (row(b, gi, i), 2)),
            pl.BlockSpec((tq, GW), lambda b, gi, i: (row(b, gi, i), gi)),
        ],
        out_specs=pl.BlockSpec((tq, GW), lambda b, gi, i: (row(b, gi, i), gi)),
        out_shape=jax.ShapeDtypeStruct((n, g * GW), BF16),
        scratch_shapes=[pltpu.VMEM((GW, r4 * tq), BF16), pltpu.VMEM((LANES, tq), F32)] + 2 * [
            pltpu.VMEM((tk, r4 * tq), F32),
            pltpu.VMEM((tk, tq), F32),
            pltpu.VMEM((tk, r4 * tq), BF16),
        ],
        compiler_params=_cparams("parallel", "parallel", "parallel"),
        name="nsa_attn",
    )(qkv, qkv, vt(1), qkv, vt(3), sel_t, e3t, cmask, aux, oc)


def _out_proj_kernel(alpha, a_ref, x_ref, gm_ref, w_ref, lg_ref, lb_ref, o_ref):
    y = jnp.dot(a_ref[...], w_ref[...], preferred_element_type=F32)
    o_ref[...] = _layer_norm(alpha * x_ref[...] + (1.0 + gm_ref[0]) * y, lg_ref[...], lb_ref[...])


def out_proj_ln(a_bf, x2, mod3, chunk, w_bf, ln_g, ln_b, seq, alpha, tm=512):
    n, d = x2.shape
    k = a_bf.shape[1]
    tpb = seq // tm
    return pl.pallas_call(
        functools.partial(_out_proj_kernel, alpha),
        grid=(n // tm,),
        in_specs=[
            pl.BlockSpec((tm, k), lambda i: (i, 0)),
            pl.BlockSpec((tm, d), lambda i: (i, 0)),
            _mod_spec(d, tpb, chunk, 1),
            pl.BlockSpec((k, d), lambda i: (0, 0)),
            pl.BlockSpec((1, d), lambda i: (0, 0)),
            pl.BlockSpec((1, d), lambda i: (0, 0)),
        ],
        out_specs=pl.BlockSpec((tm, d), lambda i: (i, 0)),
        out_shape=jax.ShapeDtypeStruct((n, d), F32),
        compiler_params=_cparams("parallel"),
        name="out_proj_ln",
    )(a_bf, x2, mod3, w_bf, _row(ln_g), _row(ln_b))


def _nsa_constants(seq, tk):
    nc = seq // CMP_STRIDE
    ns = seq // SLC_BLOCK
    sub = SLC_BLOCK // CMP_STRIDE
    seg = jnp.arange(nc)
    c2s_t = sum(((seg[None, :] + o) // sub == jnp.arange(ns)[:, None]).astype(F32)
                for o in range(CMP_BLOCK // CMP_STRIDE))
    key = jnp.arange(seq).reshape(seq // tk, tk, 1)
    e3 = (key // SLC_BLOCK == jnp.arange(ns)[None, None, :])
    return c2s_t, e3.astype(BF16)


def nsa_layer(x2, mod3, positions, w_in, cmp_pos, cmp_w1, cmp_w2, w_out, ln_g, ln_b, seq, alpha,
              tq=256, tk=256, tq_cmp=512):
    n, d = x2.shape
    bsz = n // seq
    g, dh = NSA_KV_GROUPS, NSA_HEAD_DIM
    qw, kvw = NSA_HEADS * dh, g * dh
    cols = lambda k: w_in[:, qw + k * kvw: qw + (k + 1) * kvw]
    gl = w_in[:, qw + 6 * kvw:]
    w_main = jnp.concatenate([w_in[:, :qw], cols(2), cols(3), cols(4), cols(5)], axis=1).astype(BF16)
    w_aux = jnp.concatenate([cols(0), cols(1), jnp.pad(gl, ((0, 0), (0, GW - gl.shape[1])))],
                            axis=1).astype(BF16)
    tables = _rope_tables(positions)
    nqt = qw // GW
    kinds_main = (ROPE,) * nqt + (ROPE, PLAIN, ROPE, PLAIN)
    qkv = nsa_proj(x2, mod3, w_main, tables, kinds_main, BF16, seq, "nsa_proj_main")
    aux = nsa_proj(x2, mod3, w_aux, tables, (PLAIN, PLAIN, SIGMOID), F32, seq, "nsa_proj_aux")

    hid = cmp_w1.shape[-1]
    eye = jnp.eye(g, dtype=F32)
    w1bd = jnp.einsum('spdh,gk->spgdkh', cmp_w1.reshape(2, CMP_BLOCK, dh, hid), eye)
    w1bd = w1bd.reshape(2, CMP_BLOCK, g * dh, g * hid).astype(BF16)
    w2bd = jnp.einsum('shd,gk->sghkd', cmp_w2, eye).reshape(2, g * hid, g * dh).astype(BF16)
    pos_t = jnp.tile(cmp_pos, (1, 1, g))
    nh = seq // CMP_STRIDE
    cmp_positions = jnp.pad(positions[:, CMP_BLOCK - 1::CMP_STRIDE], ((0, 0), (0, 1)))[:, :nh]
    ctabs = tuple(t.reshape(bsz, nh, LANES) for t in _rope_tables(cmp_positions))
    kvcmp = nsa_compress(aux, w1bd, w2bd, pos_t[:, :CMP_STRIDE], pos_t[:, CMP_STRIDE:], ctabs, bsz, seq)

    c2s_t, e3 = _nsa_constants(seq, tk)
    oc, sel_t = nsa_cmp_sel(qkv, kvcmp, aux, c2s_t, bsz, seq, tq_cmp)
    o = nsa_attn(qkv, aux, sel_t, oc, e3, bsz, seq, tq, tk)
    return out_proj_ln(o, x2, mod3, 2, w_out.astype(BF16), ln_g, ln_b, seq, alpha)


MOE_ROWS = 1024
RINFO = 8
DMA_UNROLL = 8


def _router_kernel(x_ref, sh_ref, sc_ref, r_ref, tril_ref, info_ref, cnt_ref, carry_ref):
    tm = x_ref.shape[0]
    i = pl.program_id(0)

    @pl.when(i == 0)
    def _():
        carry_ref[...] = jnp.zeros_like(carry_ref)

    u = x_ref[...] * (1.0 + sc_ref[0]) + sh_ref[0]
    r = r_ref[...]
    u_hi, r_hi = u.astype(BF16), r.astype(BF16)
    u_lo, r_lo = (u - u_hi.astype(F32)).astype(BF16), (r - r_hi.astype(F32)).astype(BF16)
    logits = jnp.dot(u_hi, r_hi, preferred_element_type=F32) + (
        jnp.dot(u_hi, r_lo, preferred_element_type=F32) + jnp.dot(u_lo, r_hi, preferred_element_type=F32))
    lane = lax.broadcasted_iota(jnp.int32, (1, LANES), 1)
    logits = jnp.where(lane < N_EXPERTS, logits, -jnp.inf)
    m1 = jnp.max(logits, axis=-1, keepdims=True)
    e1 = jnp.min(jnp.where(logits == m1, lane, LANES), axis=-1, keepdims=True)
    rest = jnp.where(lane == e1, -jnp.inf, logits)
    m2 = jnp.max(rest, axis=-1, keepdims=True)
    e2 = jnp.min(jnp.where(rest == m2, lane, LANES), axis=-1, keepdims=True)
    ex = jnp.exp(m2 - m1)
    g1 = 1.0 / (1.0 + ex)
    g2 = ex / (1.0 + ex)
    o1 = lane == e1
    o2 = lane == e2
    tot = jnp.where(o1 | o2, 1.0, 0.0)
    before = carry_ref[...] + jnp.dot(tril_ref[...], tot.astype(BF16), preferred_element_type=F32)
    rank1 = jnp.sum(jnp.where(o1, before, 0.0), axis=-1, keepdims=True)
    rank2 = jnp.sum(jnp.where(o2, before, 0.0), axis=-1, keepdims=True)
    carry_ref[...] += jnp.sum(tot, axis=0, keepdims=True)
    cnt_ref[...] = carry_ref[...]
    li = lax.broadcasted_iota(jnp.int32, (1, RINFO), 1)
    info = jnp.zeros((tm, RINFO), F32)
    for k, v in enumerate((e1.astype(F32), e2.astype(F32), g1, g2, rank1, rank2)):
        info = jnp.where(li == k, v, info)
    info_ref[...] = info


def moe_router(x2, mod3, router, seq, tm=512):
    n, d = x2.shape
    tpb = seq // tm
    r_pad = jnp.pad(router, ((0, 0), (0, LANES - router.shape[1])))
    tril = (jnp.arange(tm)[:, None] > jnp.arange(tm)[None, :]).astype(BF16)
    return pl.pallas_call(
        _router_kernel,
        grid=(n // tm,),
        in_specs=[
            pl.BlockSpec((tm, d), lambda i: (i, 0)),
            _mod_spec(d, tpb, 3, 1), _mod_spec(d, tpb, 4, 1),
            pl.BlockSpec((d, LANES), lambda i: (0, 0)),
            pl.BlockSpec((tm, tm), lambda i: (0, 0)),
        ],
        out_specs=[
            pl.BlockSpec((tm, RINFO), lambda i: (i, 0)),
            pl.BlockSpec((1, LANES), lambda i: (0, 0)),
        ],
        out_shape=[
            jax.ShapeDtypeStruct((n, RINFO), F32),
            jax.ShapeDtypeStruct((1, LANES), F32),
        ],
        scratch_shapes=[pltpu.VMEM((1, LANES), F32)],
        compiler_params=_cparams("arbitrary"),
        name="moe_router",
    )(x2, mod3, mod3, r_pad, tril)


def _row_dmas(tm, make_copy):
    def run(method):
        def body(r, carry):
            for k in range(TOP_K):
                getattr(make_copy(r, k), method)()
            return carry
        lax.fori_loop(0, tm, body, 0, unroll=DMA_UNROLL)
    return functools.partial(run, "start"), functools.partial(run, "wait")


def _dispatch_kernel(dest_ref, prev_ref, x_ref, sh_ref, sc_ref, xs_in_ref, xs_ref, u_ref, sem):
    del xs_in_ref
    tm = x_ref.shape[0]
    i = pl.program_id(0)
    slot = i % 2

    def copies(d_ref, s):
        return _row_dmas(tm, lambda r, k: pltpu.make_async_copy(
            u_ref.at[s, pl.ds(r, 1)], xs_ref.at[pl.ds(d_ref[TOP_K * r + k], 1)], sem.at[s]))

    u_ref[slot] = x_ref[...] * (1.0 + sc_ref[0]) + sh_ref[0]
    start_cur, wait_cur = copies(dest_ref, slot)
    _, wait_prev = copies(prev_ref, 1 - slot)
    start_cur()

    @pl.when(i > 0)
    def _():
        wait_prev()

    @pl.when(i == pl.num_programs(0) - 1)
    def _():
        wait_cur()


def moe_dispatch(x2, mod3, dest, cap, seq, tm=512):
    n, d = x2.shape
    tpb = seq // tm
    xs0 = jnp.zeros((cap, d), F32)
    return pl.pallas_call(
        _dispatch_kernel,
        grid=(n // tm,),
        in_specs=[
            pl.BlockSpec((TOP_K * tm,), lambda i: (i,), memory_space=pltpu.SMEM),
            pl.BlockSpec((TOP_K * tm,), lambda i: (jnp.maximum(i - 1, 0),), memory_space=pltpu.SMEM),
            pl.BlockSpec((tm, d), lambda i: (i, 0)),
            _mod_spec(d, tpb, 3, 1), _mod_spec(d, tpb, 4, 1),
            pl.BlockSpec(memory_space=pl.ANY),
        ],
        out_specs=pl.BlockSpec(memory_space=pl.ANY),
        out_shape=jax.ShapeDtypeStruct((cap, d), F32),
        scratch_shapes=[pltpu.VMEM((2, tm, d), F32), pltpu.SemaphoreType.DMA((2,))],
        input_output_aliases={5: 0},
        compiler_params=_cparams("arbitrary"),
        name="moe_dispatch",
    )(dest, dest, x2, mod3, mod3, xs0)


def _moe_ffn_kernel(bexp_ref, nused_ref, xs_ref, w1_ref, w3_ref, w2_ref, o_ref, xb_ref):
    del bexp_ref
    f = pl.program_id(1)
    used = pl.program_id(0) < nused_ref[0]

    @pl.when(f == 0)
    def _():
        o_ref[...] = jnp.zeros(o_ref.shape, o_ref.dtype)

    @pl.when(used & (f == 0))
    def _():
        xb_ref[...] = xs_ref[...].astype(BF16)

    @pl.when(used)
    def _():
        _swiglu_accumulate(xb_ref, w1_ref[0, 0].astype(BF16), w3_ref[0, 0].astype(BF16),
                           w2_ref[0, 0].astype(BF16), o_ref)


def moe_ffn(xs, blk_exp, n_used, w1, w3, w2, layer, tf=512):
    cap, d = xs.shape
    ff = w1.shape[3]
    grid_spec = pltpu.PrefetchScalarGridSpec(
        num_scalar_prefetch=2,
        grid=(cap // MOE_ROWS, ff // tf),
        in_specs=[
            pl.BlockSpec((MOE_ROWS, d), lambda i, f, be, nu: (i, 0)),
            pl.BlockSpec((1, 1, d, tf), lambda i, f, be, nu: (layer, be[i], 0, f)),
            pl.BlockSpec((1, 1, d, tf), lambda i, f, be, nu: (layer, be[i], 0, f)),
            pl.BlockSpec((1, 1, tf, d), lambda i, f, be, nu: (layer, be[i], f, 0)),
        ],
        out_specs=pl.BlockSpec((MOE_ROWS, d), lambda i, f, be, nu: (i, 0)),
        scratch_shapes=[pltpu.VMEM((MOE_ROWS, d), BF16)],
    )
    return pl.pallas_call(
        _moe_ffn_kernel,
        grid_spec=grid_spec,
        out_shape=jax.ShapeDtypeStruct((cap, d), F32),
        compiler_params=_cparams("parallel", "arbitrary"),
        name="moe_ffn",
    )(blk_exp, n_used, xs, w1, w3, w2)


def _combine_kernel(alpha, dest_ref, next_ref, info_ref, x_ref, gf_ref, ys_ref, lg_ref, lb_ref, o_ref,
                    buf_ref, sem):
    tm = x_ref.shape[0]
    i = pl.program_id(0)
    slot = i % 2

    def copies(d_ref, s):
        return _row_dmas(tm, lambda r, k: pltpu.make_async_copy(
            ys_ref.at[pl.ds(d_ref[TOP_K * r + k], 1)], buf_ref.at[s, k, pl.ds(r, 1)], sem.at[s]))

    start_cur, wait_cur = copies(dest_ref, slot)
    start_next, _ = copies(next_ref, 1 - slot)

    @pl.when(i == 0)
    def _():
        start_cur()

    @pl.when(i + 1 < pl.num_programs(0))
    def _():
        start_next()

    wait_cur()
    info = info_ref[...]
    y = info[:, 2:3] * buf_ref[slot, 0] + info[:, 3:4] * buf_ref[slot, 1]
    o_ref[...] = _layer_norm(alpha * x_ref[...] + (1.0 + gf_ref[0]) * y, lg_ref[...], lb_ref[...])


def moe_combine(ys, dest, info, x2, mod3, ln_g, ln_b, seq, alpha, tm=512):
    n, d = x2.shape
    tpb = seq // tm
    return pl.pallas_call(
        functools.partial(_combine_kernel, alpha),
        grid=(n // tm,),
        in_specs=[
            pl.BlockSpec((TOP_K * tm,), lambda i: (i,), memory_space=pltpu.SMEM),
            pl.BlockSpec((TOP_K * tm,), lambda i: (jnp.minimum(i + 1, n // tm - 1),), memory_space=pltpu.SMEM),
            pl.BlockSpec((tm, RINFO), lambda i: (i, 0)),
            pl.BlockSpec((tm, d), lambda i: (i, 0)),
            _mod_spec(d, tpb, 5, 1),
            pl.BlockSpec(memory_space=pl.ANY),
            pl.BlockSpec((1, d), lambda i: (0, 0)),
            pl.BlockSpec((1, d), lambda i: (0, 0)),
        ],
        out_specs=pl.BlockSpec((tm, d), lambda i: (i, 0)),
        out_shape=jax.ShapeDtypeStruct((n, d), F32),
        scratch_shapes=[pltpu.VMEM((2, TOP_K, tm, d), F32), pltpu.SemaphoreType.DMA((2,))],
        compiler_params=_cparams("arbitrary"),
        name="moe_combine",
    )(dest, dest, info, x2, mod3, ys, _row(ln_g), _row(ln_b))


def moe_layer(x2, mod3, router, w1, w3, w2, layer, ln_g, ln_b, seq, alpha):
    n, d = x2.shape
    info, counts = moe_router(x2, mod3, router, seq)
    sizes = counts[0, :N_EXPERTS].astype(jnp.int32)
    padded = (sizes + MOE_ROWS - 1) // MOE_ROWS * MOE_ROWS
    pend = jnp.cumsum(padded)
    pstart = pend - padded
    cap = n * TOP_K + N_EXPERTS * MOE_ROWS
    n_blk = cap // MOE_ROWS
    blk_lo = jnp.arange(n_blk, dtype=jnp.int32) * MOE_ROWS
    blk_exp = jnp.minimum(jnp.sum(blk_lo[:, None] >= pend[None, :], axis=1), N_EXPERTS - 1).astype(jnp.int32)
    n_used = (pend[-1:] // MOE_ROWS).astype(jnp.int32)
    ids = jnp.arange(N_EXPERTS, dtype=jnp.int32)
    e = info[:, 0:TOP_K].astype(jnp.int32)
    start = jnp.sum(jnp.where(e[:, :, None] == ids, pstart, 0), axis=-1)
    dest = (start + info[:, 4:4 + TOP_K].astype(jnp.int32)).reshape(-1)
    xs = moe_dispatch(x2, mod3, dest, cap, seq)
    ys = moe_ffn(xs, blk_exp, n_used, w1, w3, w2, layer)
    return moe_combine(ys, dest, info, x2, mod3, ln_g, ln_b, seq, alpha)


def kernel(x, c, positions, ada_w, ada_b, ln_g, ln_b, conv_w_in, conv_b_in, conv_dw, conv_dw_b, conv_ln_g, conv_ln_b, conv_w_out, conv_b_out, nsa_w_in, nsa_cmp_pos, nsa_cmp_w1, nsa_cmp_w2, nsa_w_out, pool_w, pool_scale, ffn_w1, ffn_w3, ffn_w2, moe_router, moe_w1, moe_w3, moe_w2):
    bsz, seq, d = x.shape
    depth = ada_w.shape[0]
    alpha = (2 * depth) ** 0.25
    mod = ada_mod(c, ada_w, ada_b)
    x2 = x.reshape(bsz * seq, d)
    mods = mod.reshape(depth, bsz, 1, 6 * d)
    for i in range(depth):
        mod3 = mods[i]
        j = i // N_MIXERS
        if i % N_MIXERS == 0:
            h = conv_in(x2, mod3, conv_w_in[j].astype(BF16), conv_b_in[j], seq)
            x2 = conv_tail(h, x2, mod3, conv_dw[j], conv_dw_b[j], conv_ln_g[j], conv_ln_b[j],
                           conv_w_out[j].astype(BF16), conv_b_out[j], ln_g[i, 0], ln_b[i, 0], seq, alpha)
        elif i % N_MIXERS == 1:
            x2 = nsa_layer(x2, mod3, positions, nsa_w_in[j], nsa_cmp_pos[j], nsa_cmp_w1[j], nsa_cmp_w2[j],
                           nsa_w_out[j], ln_g[i, 0], ln_b[i, 0], seq, alpha)
        else:
            x2 = pool_layer(x2, mod3, pool_w[j].astype(BF16), pool_scale[j], ln_g[i, 0], ln_b[i, 0],
                            seq, alpha)
        j = i // 2
        if i % 2 == 0:
            x2 = ffn_dense(x2, mod3, ffn_w1, ffn_w3, ffn_w2, j, ln_g[i, 1], ln_b[i, 1], seq, alpha)
        else:
            x2 = moe_layer(x2, mod3, moe_router[j], moe_w1, moe_w3, moe_w2, j,
                           ln_g[i, 1], ln_b[i, 1], seq, alpha)
    return x2.reshape(bsz, seq, d)
```
